```python
import jax, jax.numpy as jnp
from jax import lax
import numpy as np

D_MODEL = 2048
BATCH = 2
SEQ = 4096
DEPTH = 4
DEC_BATCH = 32
DEC_SEQ = 8
PAST_LEN = 16384
PAGE_SIZE = 128

HEAD_DIM = 64
N_A_LAYERS = DEPTH // 2
N_B_LAYERS = DEPTH - N_A_LAYERS
A_HEADS = D_MODEL // HEAD_DIM
A_KV_HEADS = A_HEADS // 8
A_WINDOW = 128
B_HEADS = D_MODEL // HEAD_DIM
B_KV_HEADS = B_HEADS // 8
B_GROUPS = ((128, 1), (512, 4), (2048, 16))
B_N_GROUPS = len(B_GROUPS)
B_MAX_WINDOW = max(w for w, _ in B_GROUPS)
D_FF = 4 * D_MODEL
ROPE_THETA = 10000.0
LN_EPS = 1e-5
BLOCK = 128
DEEPNORM_ALPHA = (2 * DEPTH) ** 0.25
DEEPNORM_BETA = (8 * DEPTH) ** -0.25
NEG_INF = -1e30

kernel_name = 'dilated_yoco_hybrid_step'


def _layernorm(x, g, b):
    x32 = x.astype(jnp.float32)
    mu = x32.mean(-1, keepdims=True)
    var = jnp.square(x32 - mu).mean(-1, keepdims=True)
    y = (x32 - mu) * lax.rsqrt(var + LN_EPS) * g.astype(jnp.float32) + b.astype(jnp.float32)
    return y.astype(x.dtype)


def _rope(x, pos):
    half = HEAD_DIM // 2
    inv = ROPE_THETA ** (-jnp.arange(half, dtype=jnp.float32) / half)
    ang = pos.astype(jnp.float32)[:, None] * inv[None, :]
    cos = jnp.cos(ang)[None, :, None, :]
    sin = jnp.sin(ang)[None, :, None, :]
    x32 = x.astype(jnp.float32)
    x1, x2 = x32[..., :half], x32[..., half:]
    return jnp.concatenate([x1 * cos - x2 * sin, x2 * cos + x1 * sin], axis=-1).astype(x.dtype)


def _probs(s, sink=None):
    m = jnp.max(s, axis=-1, keepdims=True)
    if sink is not None:
        m = jnp.maximum(m, sink)
    e = jnp.exp(s - m)
    den = jnp.sum(e, axis=-1, keepdims=True)
    if sink is not None:
        den = den + jnp.exp(sink - m)
    return e / den, (jnp.log(den) + m)[..., 0]


def _banded_attn(q, k, v, n_back, sink=None):
    n, length, h, hd = q.shape
    kvh = k.shape[2]
    g = h // kvh
    nb = -(-length // BLOCK)
    pad = ((0, 0), (0, nb * BLOCK - length), (0, 0), (0, 0))
    qb = jnp.pad(q, pad).reshape(n, nb, BLOCK, kvh, g, hd)
    kb = jnp.pad(k, pad).reshape(n, nb, BLOCK, kvh, hd)
    vb = jnp.pad(v, pad).reshape(n, nb, BLOCK, kvh, hd)
    prev = lambda t: jnp.pad(t, ((0, 0), (1, 0), (0, 0), (0, 0), (0, 0)))[:, :-1]
    kk = jnp.concatenate([prev(kb), kb], axis=2)
    vv = jnp.concatenate([prev(vb), vb], axis=2)
    s = jnp.einsum('bnqhgd,bnkhd->bnhgqk', qb, kk, preferred_element_type=jnp.float32) * hd ** -0.5
    qi = jnp.arange(BLOCK)[:, None]
    ki = jnp.arange(2 * BLOCK)[None, :]
    dist = BLOCK + qi - ki
    kpos = jnp.arange(nb)[:, None, None] * BLOCK + ki[None] - BLOCK
    valid = (dist >= 0) & (dist <= n_back) & (kpos >= 0)
    s = jnp.where(valid[None, :, None, None], s, NEG_INF)
    if sink is not None:
        sink = sink.astype(jnp.float32).reshape(kvh, g)[None, None, :, :, None, None]
    p, lse = _probs(s, sink)
    o = jnp.einsum('bnhgqk,bnkhd->bnqhgd', p.astype(v.dtype), vv)
    o = o.reshape(n, nb * BLOCK, h, hd)[:, :length]
    lse = lse.transpose(0, 1, 4, 2, 3).reshape(n, nb * BLOCK, h)[:, :length]
    return o, lse


def _window_sample(q, kc, vc, sink):
    n, t, h, hd = q.shape
    kvh = kc.shape[2]
    g = h // kvh
    buf = kc.shape[1] - t
    s = jnp.einsum('bthgd,bkhd->bhgtk', q.reshape(n, t, kvh, g, hd), kc,
                   preferred_element_type=jnp.float32) * hd ** -0.5
    dist = (buf + jnp.arange(t))[:, None] - jnp.arange(buf + t)[None, :]
    valid = (dist >= 0) & (dist < A_WINDOW)
    s = jnp.where(valid, s, NEG_INF)
    p, _ = _probs(s, sink.astype(jnp.float32).reshape(kvh, g)[None, :, :, None, None])
    return jnp.einsum('bhgtk,bkhd->bthgd', p.astype(vc.dtype), vc).reshape(n, t, h, hd)


def _by_residue(x, d):
    n, s = x.shape[:2]
    rest = x.shape[2:]
    return jnp.moveaxis(x.reshape(n, s // d, d, *rest), 2, 1).reshape(n * d, s // d, *rest)


def _from_residue(x, d, n):
    ls = x.shape[1]
    rest = x.shape[2:]
    return jnp.moveaxis(x.reshape(n, d, ls, *rest), 1, 2).reshape(n, d * ls, *rest)


def _combine_groups(outs, lses):
    w = jax.nn.softmax(jnp.stack(lses, axis=0), axis=0)
    return jnp.einsum('gbth,gbthd->bthd', w.astype(outs[0].dtype), jnp.stack(outs, axis=0))


def _dilated_prompt(q, k, v):
    n = q.shape[0]
    outs, lses = [], []
    for gi, (window, d) in enumerate(B_GROUPS):
        o, lse = _banded_attn(_by_residue(q[:, :, gi], d), _by_residue(k, d), _by_residue(v, d), window // d)
        outs.append(_from_residue(o, d, n))
        lses.append(_from_residue(lse, d, n))
    return _combine_groups(outs, lses)


def _dilated_sample(q, kc, vc):
    n, t, _, h, hd = q.shape
    kvh = kc.shape[2]
    g = h // kvh
    buf = kc.shape[1] - t
    outs, lses = [], []
    for gi, (window, d) in enumerate(B_GROUPS):
        n_keys = window // d + 1
        idx = buf + jnp.arange(t)[:, None] - jnp.arange(n_keys)[None, :] * d
        valid = idx >= 0
        idx = jnp.maximum(idx, 0)
        kg = kc[:, idx]
        vg = vc[:, idx]
        s = jnp.einsum('bthgd,btjhd->bhgtj', q[:, :, gi].reshape(n, t, kvh, g, hd), kg,
                       preferred_element_type=jnp.float32) * hd ** -0.5
        s = jnp.where(valid, s, NEG_INF)
        p, lse = _probs(s)
        outs.append(jnp.einsum('bhgtj,btjhd->bthgd', p.astype(vg.dtype), vg).reshape(n, t, h, hd))
        lses.append(lse.transpose(0, 3, 1, 2).reshape(n, t, h))
    return _combine_groups(outs, lses)


def _trunk(x, pos, attend_a, prepare_b, attend_b, ln_g, ln_b, w_qkv_a, sinks_a, w_o_a,
           w_kv_b, w_q_b, w_o_b, w_up, w_down):
    n, t, _ = x.shape
    qa = A_HEADS * HEAD_DIM
    kva = A_KV_HEADS * HEAD_DIM
    kvb = B_KV_HEADS * HEAD_DIM
    a_states = []
    b_ctx, b_state = None, None
    for layer in range(DEPTH):
        if layer < N_A_LAYERS:
            qkv = x @ w_qkv_a[layer]
            q = _rope(qkv[..., :qa].reshape(n, t, A_HEADS, HEAD_DIM), pos)
            k = _rope(qkv[..., qa:qa + kva].reshape(n, t, A_KV_HEADS, HEAD_DIM), pos)
            v = qkv[..., qa + kva:].reshape(n, t, A_KV_HEADS, HEAD_DIM)
            o, state = attend_a(layer, q, k, v, sinks_a[layer])
            a_states.append(state)
            mix = o.reshape(n, t, qa) @ w_o_a[layer]
        else:
            if layer == N_A_LAYERS:
                kv = x @ w_kv_b
                kb = _rope(kv[..., :kvb].reshape(n, t, B_KV_HEADS, HEAD_DIM), pos)
                vb = kv[..., kvb:].reshape(n, t, B_KV_HEADS, HEAD_DIM)
                b_ctx, b_state = prepare_b(kb, vb)
            j = layer - N_A_LAYERS
            q = _rope((x @ w_q_b[j]).reshape(n, t, B_N_GROUPS * B_HEADS, HEAD_DIM), pos)
            o = attend_b(q.reshape(n, t, B_N_GROUPS, B_HEADS, HEAD_DIM), *b_ctx)
            mix = o.reshape(n, t, B_HEADS * HEAD_DIM) @ w_o_b[j]
        x = _layernorm(DEEPNORM_ALPHA * x + mix, ln_g[layer, 0], ln_b[layer, 0])
        hid = jnp.square(jax.nn.relu(x @ w_up[layer]))
        x = _layernorm(DEEPNORM_ALPHA * x + hid @ w_down[layer], ln_g[layer, 1], ln_b[layer, 1])
    return x, a_states, b_state


def setup_inputs(seed: int = 0) -> dict:
    key = jax.random.key(seed)
    ks = jax.random.split(key, 20)
    f32 = jnp.float32
    nrm = lambda k, shape, scale: jax.random.normal(k, shape, f32) * scale
    a_buf = min(A_WINDOW, PAST_LEN)
    b_buf = min(B_MAX_WINDOW, PAST_LEN)
    return {
        'x_prompt': nrm(ks[0], (BATCH, SEQ, D_MODEL), 1.0),
        'x_sample': nrm(ks[1], (DEC_BATCH, DEC_SEQ, D_MODEL), 1.0),
        'cache_a_k': nrm(ks[2], (N_A_LAYERS, DEC_BATCH, a_buf, A_KV_HEADS, HEAD_DIM), 1.0),
        'cache_a_v': nrm(ks[3], (N_A_LAYERS, DEC_BATCH, a_buf, A_KV_HEADS, HEAD_DIM), 1.0),
        'cache_b_k': nrm(ks[4], (DEC_BATCH, b_buf, B_KV_HEADS, HEAD_DIM), 1.0),
        'cache_b_v': nrm(ks[5], (DEC_BATCH, b_buf, B_KV_HEADS, HEAD_DIM), 1.0),
        'ln_g': 1.0 + nrm(ks[6], (DEPTH, 2, D_MODEL), 0.02),
        'ln_b': nrm(ks[7], (DEPTH, 2, D_MODEL), 0.02),
        'w_qkv_a': nrm(ks[8], (N_A_LAYERS, D_MODEL, (A_HEADS + 2 * A_KV_HEADS) * HEAD_DIM), D_MODEL ** -0.5),
        'sinks_a': nrm(ks[9], (N_A_LAYERS, A_HEADS), 0.5),
        'w_o_a': nrm(ks[10], (N_A_LAYERS, A_HEADS * HEAD_DIM, D_MODEL), DEEPNORM_BETA * (A_HEADS * HEAD_DIM) ** -0.5),
        'w_kv_b': nrm(ks[11], (D_MODEL, 2 * B_KV_HEADS * HEAD_DIM), D_MODEL ** -0.5),
        'w_q_b': nrm(ks[12], (N_B_LAYERS, D_MODEL, B_N_GROUPS * B_HEADS * HEAD_DIM), D_MODEL ** -0.5),
        'w_o_b': nrm(ks[13], (N_B_LAYERS, B_HEADS * HEAD_DIM, D_MODEL), DEEPNORM_BETA * (B_HEADS * HEAD_DIM) ** -0.5),
        'w_up': nrm(ks[14], (DEPTH, D_MODEL, D_FF), D_MODEL ** -0.5),
        'w_down': nrm(ks[15], (DEPTH, D_FF, D_MODEL), DEEPNORM_BETA * D_FF ** -0.5),
    }


def reference(x_prompt, x_sample, cache_a_k, cache_a_v, cache_b_k, cache_b_v, ln_g, ln_b,
              w_qkv_a, sinks_a, w_o_a, w_kv_b, w_q_b, w_o_b, w_up, w_down):
    weights = (ln_g, ln_b, w_qkv_a, sinks_a, w_o_a, w_kv_b, w_q_b, w_o_b, w_up, w_down)

    def prompt_a(layer, q, k, v, sink):
        o, _ = _banded_attn(q, k, v, A_WINDOW - 1, sink)
        keep = min(A_WINDOW, k.shape[1])
        return o, (k[:, -keep:], v[:, -keep:])

    def sample_a(layer, q, k, v, sink):
        kc = jnp.concatenate([cache_a_k[layer], k], axis=1)
        vc = jnp.concatenate([cache_a_v[layer], v], axis=1)
        keep = cache_a_k.shape[2]
        return _window_sample(q, kc, vc, sink), (kc[:, -keep:], vc[:, -keep:])

    def prompt_b(k, v):
        keep = min(B_MAX_WINDOW, k.shape[1])
        return (k, v), (k[:, -keep:], v[:, -keep:])

    def sample_b(k, v):
        kc = jnp.concatenate([cache_b_k, k], axis=1)
        vc = jnp.concatenate([cache_b_v, v], axis=1)
        keep = cache_b_k.shape[1]
        return (kc, vc), (kc[:, -keep:], vc[:, -keep:])

    pos_prompt = jnp.arange(x_prompt.shape[1], dtype=jnp.int32)
    pos_sample = PAST_LEN + jnp.arange(x_sample.shape[1], dtype=jnp.int32)
    y_prompt, a_prompt, b_prompt = _trunk(x_prompt, pos_prompt, prompt_a, prompt_b, _dilated_prompt, *weights)
    y_sample, a_sample, b_sample = _trunk(x_sample, pos_sample, sample_a, sample_b, _dilated_sample, *weights)
    new_a_k_prompt = jnp.stack([s[0] for s in a_prompt], axis=0)
    new_a_v_prompt = jnp.stack([s[1] for s in a_prompt], axis=0)
    new_a_k_sample = jnp.stack([s[0] for s in a_sample], axis=0)
    new_a_v_sample = jnp.stack([s[1] for s in a_sample], axis=0)
    new_b_k_prompt, new_b_v_prompt = b_prompt
    new_b_k_sample, new_b_v_sample = b_sample
    return (y_prompt, y_sample, new_a_k_prompt, new_a_v_prompt, new_a_k_sample, new_a_v_sample,
            new_b_k_prompt, new_b_v_prompt, new_b_k_sample, new_b_v_sample)
```

```python
import functools

import jax
import jax.numpy as jnp
from jax import lax
from jax.experimental import pallas as pl
from jax.experimental.pallas import tpu as pltpu

HEAD_DIM = 64
GQA = 8
A_WINDOW = 128
B_GROUPS = ((128, 1), (512, 4), (2048, 16))
PAST_LEN = 16384
ROPE_THETA = 10000.0
LN_EPS = 1e-5
BLOCK = 128
NEG_INF = -1e30
LANES = 128
VMEM_LIMIT = 56 * 1024 * 1024

LN_ROWS = 16

BF16 = jnp.bfloat16
F32 = jnp.float32


def _mod(x, n):
    assert n & (n - 1) == 0, "power-of-two modulus only"
    return x & (n - 1)


def _div(x, n):
    assert n & (n - 1) == 0, "power-of-two divisor only"
    return x >> (n.bit_length() - 1)


def _params(*sem):
    return pltpu.CompilerParams(dimension_semantics=sem, vmem_limit_bytes=VMEM_LIMIT)


def _pick(n, pref):
    if n <= pref:
        return n
    t = pref
    while n % t:
        t //= 2
    return t


def _rope_chunk(x, cos, sin_signed):
    lane = lax.broadcasted_iota(jnp.int32, x.shape, 1)
    first = _mod(lane, HEAD_DIM) < (HEAD_DIM // 2)
    partner = jnp.where(first, pltpu.roll(x, LANES - HEAD_DIM // 2, 1), pltpu.roll(x, HEAD_DIM // 2, 1))
    return x * cos + partner * sin_signed


def _proj_kernel(*refs, rope_chunks, scale, relu2):
    if rope_chunks:
        a_ref, w_ref, cos_ref, sin_ref, o_ref = refs
    else:
        a_ref, w_ref, o_ref = refs
    acc = jnp.dot(a_ref[...].astype(BF16), w_ref[...], preferred_element_type=F32)
    if relu2:
        acc = jnp.square(jnp.maximum(acc, 0.0))
    if not rope_chunks:
        o_ref[...] = acc.astype(o_ref.dtype)
        return
    cos = cos_ref[...]
    sin = sin_ref[...]
    for c in range(acc.shape[1] // LANES):
        x = acc[:, c * LANES:(c + 1) * LANES]
        if c < rope_chunks:
            x = _rope_chunk(x, cos, sin)
            if scale != 1.0:
                x = x * scale
        o_ref[:, c * LANES:(c + 1) * LANES] = x.astype(o_ref.dtype)


def _proj(a, w, out_dtype, *, rope=None, rope_cols=0, scale=1.0, relu2=False, tm=1024, tn=1024):
    m, k = a.shape
    n = w.shape[1]
    tm = _pick(m, tm)
    tn = _pick(n, tn)
    in_specs = [pl.BlockSpec((tm, k), lambda j, i: (i, 0)), pl.BlockSpec((k, tn), lambda j, i: (0, j))]
    args = [a, w]
    rope_chunks = 0
    if rope is not None:
        cos, sin = rope
        p = cos.shape[0]
        tm = _pick(p, tm)
        pb = p // tm
        in_specs[0] = pl.BlockSpec((tm, k), lambda j, i: (i, 0))
        in_specs += [pl.BlockSpec((tm, LANES), lambda j, i: (i % pb, 0))] * 2
        args += [cos, sin]
        assert rope_cols == n or tn == n
        rope_chunks = min(rope_cols, tn) // LANES
    return pl.pallas_call(
        functools.partial(_proj_kernel, rope_chunks=rope_chunks, scale=scale, relu2=relu2),
        grid=(n // tn, m // tm),
        in_specs=in_specs,
        out_specs=pl.BlockSpec((tm, tn), lambda j, i: (i, j)),
        out_shape=jax.ShapeDtypeStruct((m, n), out_dtype),
        compiler_params=_params("parallel", "arbitrary"),
        name="proj",
    )(*args)


def _res_ln_kernel(a_ref, w_ref, x_ref, g_ref, b_ref, o32_ref, o16_ref, acc_ref, *, alpha, nk):
    kk = pl.program_id(1)
    part = jnp.dot(a_ref[...].astype(BF16), w_ref[...], preferred_element_type=F32)

    @pl.when(kk == 0)
    def _():
        acc_ref[...] = part

    @pl.when(kk > 0)
    def _():
        acc_ref[...] += part

    @pl.when(kk == nk - 1)
    def _():
        gain = g_ref[...]
        bias = b_ref[...]

        def rows_body(r, carry):
            rows = pl.ds(pl.multiple_of(r * LN_ROWS, LN_ROWS), LN_ROWS)
            v = alpha * x_ref[rows, :] + acc_ref[rows, :]
            mu = jnp.mean(v, axis=-1, keepdims=True)
            c = v - mu
            var = jnp.mean(c * c, axis=-1, keepdims=True)
            y = c * lax.rsqrt(var + LN_EPS) * gain + bias
            o32_ref[rows, :] = y
            o16_ref[rows, :] = y.astype(BF16)
            return carry

        lax.fori_loop(0, acc_ref.shape[0] // LN_ROWS, rows_body, 0)


def _res_ln(a, w, x, g, b, alpha, *, tm=512, tk=1024):
    m, k = a.shape
    n = w.shape[1]
    tm = _pick(m, tm)
    tk = _pick(k, tk)
    nk = k // tk
    return pl.pallas_call(
        functools.partial(_res_ln_kernel, alpha=alpha, nk=nk),
        grid=(m // tm, nk),
        in_specs=[
            pl.BlockSpec((tm, tk), lambda i, kk: (i, kk)),
            pl.BlockSpec((tk, n), lambda i, kk: (kk, 0)),
            pl.BlockSpec((tm, n), lambda i, kk: (i, 0)),
            pl.BlockSpec((1, n), lambda i, kk: (0, 0)),
            pl.BlockSpec((1, n), lambda i, kk: (0, 0)),
        ],
        out_specs=[pl.BlockSpec((tm, n), lambda i, kk: (i, 0)), pl.BlockSpec((tm, n), lambda i, kk: (i, 0))],
        out_shape=[jax.ShapeDtypeStruct((m, n), F32), jax.ShapeDtypeStruct((m, n), BF16)],
        scratch_shapes=[pltpu.VMEM((tm, n), F32)],
        compiler_params=_params("parallel", "arbitrary"),
        name="res_ln",
    )(a, w, x, g.reshape(1, n), b.reshape(1, n))


def _band_kernel(*refs, n_back, kvh, has_sink, with_lse):
    refs = list(refs)
    sink_ref = refs.pop(0) if has_sink else None
    q_ref, kp_ref, kc_ref, vp_ref, vc_ref, o_ref = refs[:6]
    lse_ref = refs[6] if with_lse else None
    i = pl.program_id(2)
    q = q_ref[...]
    kk = jnp.concatenate([kp_ref[...], kc_ref[...]], axis=0).astype(BF16)
    vv = jnp.concatenate([vp_ref[...], vc_ref[...]], axis=0).astype(BF16)
    qi = lax.broadcasted_iota(jnp.int32, (BLOCK, 2 * BLOCK), 0)
    ki = lax.broadcasted_iota(jnp.int32, (BLOCK, 2 * BLOCK), 1)
    dist = BLOCK + qi - ki
    valid = (dist >= 0) & (dist <= n_back) & ((ki >= BLOCK) | (i > 0))
    if with_lse:
        lane = lax.broadcasted_iota(jnp.int32, (BLOCK, LANES), 1)
        lse_tile = jnp.zeros((BLOCK, LANES), F32)
    for h in range(kvh):
        k_h = kk[:, h * HEAD_DIM:(h + 1) * HEAD_DIM]
        v_h = vv[:, h * HEAD_DIM:(h + 1) * HEAD_DIM]
        q_h = jnp.concatenate(
            [q[:, (h * GQA + g) * HEAD_DIM:(h * GQA + g + 1) * HEAD_DIM] for g in range(GQA)], axis=0)
        s = lax.dot_general(q_h, k_h, (((1,), (1,)), ((), ())), preferred_element_type=F32)
        es, invs = [], []
        for g in range(GQA):
            s_g = jnp.where(valid, s[g * BLOCK:(g + 1) * BLOCK], NEG_INF)
            m = jnp.max(s_g, axis=-1, keepdims=True)
            if has_sink:
                sink = sink_ref[h * GQA + g]
                m = jnp.maximum(m, sink)
            e = jnp.exp(s_g - m)
            den = jnp.sum(e, axis=-1, keepdims=True)
            if has_sink:
                den = den + jnp.exp(sink - m)
            es.append(e.astype(BF16))
            invs.append(1.0 / den)
            if with_lse:
                lse_tile = jnp.where(lane == h * GQA + g, jnp.log(den) + m, lse_tile)
        pv = jnp.dot(jnp.concatenate(es, axis=0), v_h, preferred_element_type=F32)
        for g in range(0, GQA, 2):
            pair = jnp.concatenate([pv[g * BLOCK:(g + 1) * BLOCK] * invs[g],
                                    pv[(g + 1) * BLOCK:(g + 2) * BLOCK] * invs[g + 1]], axis=1)
            col = (h * GQA + g) * HEAD_DIM
            o_ref[:, col:col + 2 * HEAD_DIM] = pair.astype(o_ref.dtype)
    if with_lse:
        lse_ref[...] = lse_tile


def _band_attn(q, k, v, d, n_back, sinks=None, with_lse=False):
    n, length, dc = q.shape
    c = dc // d
    ck = k.shape[2] // d
    kvh = ck // HEAD_DIM
    assert c == kvh * GQA * HEAD_DIM and length % BLOCK == 0 and n_back <= BLOCK
    nb = length // BLOCK
    cur = lambda b, r, i: (b, i, r)
    prev = lambda b, r, i: (b, jnp.maximum(i - 1, 0), r)
    in_specs = [
        pl.BlockSpec((None, BLOCK, c), cur),
        pl.BlockSpec((None, BLOCK, ck), prev), pl.BlockSpec((None, BLOCK, ck), cur),
        pl.BlockSpec((None, BLOCK, ck), prev), pl.BlockSpec((None, BLOCK, ck), cur),
    ]
    args = [q, k, k, v, v]
    if sinks is not None:
        in_specs = [pl.BlockSpec(memory_space=pltpu.SMEM)] + in_specs
        args = [sinks.astype(F32)] + args
    out_specs = [pl.BlockSpec((None, BLOCK, c), cur)]
    out_shape = [jax.ShapeDtypeStruct(q.shape, BF16)]
    if with_lse:
        out_specs.append(pl.BlockSpec((None, BLOCK, LANES), cur))
        out_shape.append(jax.ShapeDtypeStruct((n, length, d * LANES), F32))
    res = pl.pallas_call(
        functools.partial(_band_kernel, n_back=n_back, kvh=kvh, has_sink=sinks is not None, with_lse=with_lse),
        grid=(n, d, nb),
        in_specs=in_specs,
        out_specs=out_specs,
        out_shape=out_shape,
        compiler_params=_params("parallel", "parallel", "arbitrary"),
        name="band_attn",
    )(*args)
    return res if with_lse else res[0]


def _combine_kernel(*refs, ng):
    o_refs, l_refs, out_ref = refs[:ng], refs[ng:2 * ng], refs[2 * ng]
    c = out_ref.shape[1]
    lses = [r[...] for r in l_refs]
    top = functools.reduce(jnp.maximum, lses)
    ws = [jnp.exp(l - top) for l in lses]
    inv = 1.0 / functools.reduce(jnp.add, ws)
    head_of_col = _div(lax.broadcasted_iota(jnp.int32, (LANES, c), 1), HEAD_DIM)
    expand = (head_of_col == lax.broadcasted_iota(jnp.int32, (LANES, c), 0)).astype(BF16)
    acc = None
    for o_ref, w in zip(o_refs, ws):
        w = w * inv
        hi = w.astype(BF16)
        lo = (w - hi.astype(F32)).astype(BF16)
        wide = (jnp.dot(hi, expand, preferred_element_type=F32)
                + jnp.dot(lo, expand, preferred_element_type=F32))
        term = wide * o_ref[...].astype(F32)
        acc = term if acc is None else acc + term
    out_ref[...] = acc.astype(out_ref.dtype)


def _combine(outs, lses, tm=512):
    m, c = outs[0].shape
    tm = _pick(m, tm)
    ng = len(outs)
    return pl.pallas_call(
        functools.partial(_combine_kernel, ng=ng),
        grid=(m // tm,),
        in_specs=[pl.BlockSpec((tm, c), lambda i: (i, 0))] * ng + [pl.BlockSpec((tm, LANES), lambda i: (i, 0))] * ng,
        out_specs=pl.BlockSpec((tm, c), lambda i: (i, 0)),
        out_shape=jax.ShapeDtypeStruct((m, c), BF16),
        compiler_params=_params("parallel"),
        name="combine",
    )(*outs, *lses)


def _stack_heads(q, base, t):
    return jnp.concatenate([q[:, (base + g) * HEAD_DIM:(base + g + 1) * HEAD_DIM] for g in range(GQA)], axis=0)


def _sample_a_kernel(sink_ref, q_ref, kc_ref, kn_ref, vc_ref, vn_ref, o_ref, *, kvh, window):
    t = q_ref.shape[0]
    buf = kc_ref.shape[0]
    q = q_ref[...]
    kc = jnp.concatenate([kc_ref[...], kn_ref[...]], axis=0).astype(BF16)
    vc = jnp.concatenate([vc_ref[...], vn_ref[...]], axis=0).astype(BF16)
    rows = GQA * t
    row = lax.broadcasted_iota(jnp.int32, (rows, buf + t), 0)
    col = lax.broadcasted_iota(jnp.int32, (rows, buf + t), 1)
    dist = buf + _mod(row, t) - col
    valid = (dist >= 0) & (dist < window)
    rowg = _div(lax.broadcasted_iota(jnp.int32, (rows, 1), 0), t)
    for h in range(kvh):
        k_h = kc[:, h * HEAD_DIM:(h + 1) * HEAD_DIM]
        v_h = vc[:, h * HEAD_DIM:(h + 1) * HEAD_DIM]
        q_h = _stack_heads(q, h * GQA, t).astype(BF16)
        s = lax.dot_general(q_h, k_h, (((1,), (1,)), ((), ())), preferred_element_type=F32)
        s = jnp.where(valid, s, NEG_INF)
        sink = jnp.zeros((rows, 1), F32)
        for g in range(GQA):
            sink = jnp.where(rowg == g, sink_ref[h * GQA + g], sink)
        m = jnp.maximum(jnp.max(s, axis=-1, keepdims=True), sink)
        e = jnp.exp(s - m)
        den = jnp.sum(e, axis=-1, keepdims=True) + jnp.exp(sink - m)
        o = jnp.dot(e.astype(BF16), v_h, preferred_element_type=F32) * (1.0 / den)
        for g in range(GQA):
            col0 = (h * GQA + g) * HEAD_DIM
            o_ref[:, col0:col0 + HEAD_DIM] = o[g * t:(g + 1) * t].astype(o_ref.dtype)


def _sample_a_attn(q, kcache, knew, vcache, vnew, sinks):
    n, t, c = q.shape
    buf, ck = kcache.shape[1:]
    row3 = lambda b: (b, 0, 0)
    return pl.pallas_call(
        functools.partial(_sample_a_kernel, kvh=ck // HEAD_DIM, window=A_WINDOW),
        grid=(n,),
        in_specs=[pl.BlockSpec(memory_space=pltpu.SMEM),
                  pl.BlockSpec((None, t, c), row3),
                  pl.BlockSpec((None, buf, ck), row3), pl.BlockSpec((None, t, ck), row3),
                  pl.BlockSpec((None, buf, ck), row3), pl.BlockSpec((None, t, ck), row3)],
        out_specs=pl.BlockSpec((None, t, c), row3),
        out_shape=jax.ShapeDtypeStruct((n, t, c), F32),
        compiler_params=_params("parallel"),
        name="sample_a_attn",
    )(sinks.astype(F32), q, kcache, knew, vcache, vnew)


def _sample_b_kernel(q_ref, kc_ref, kn_ref, vc_ref, vn_ref, o_ref, *, kvh):
    t = q_ref.shape[0]
    buf = kc_ref.shape[0]
    c = o_ref.shape[1]
    rows = GQA * t
    q = q_ref[...]
    kn = kn_ref[...].astype(BF16)
    vn = vn_ref[...].astype(BF16)
    tok_n = _mod(lax.broadcasted_iota(jnp.int32, (rows, t), 0), t)
    dist_n = tok_n - lax.broadcasted_iota(jnp.int32, (rows, t), 1)
    for h in range(kvh):
        hs = slice(h * HEAD_DIM, (h + 1) * HEAD_DIM)
        kn_h, vn_h = kn[:, hs], vn[:, hs]
        outs, lses = [], []
        for gi, (window, d) in enumerate(B_GROUPS):
            start = max(buf - window, 0)
            span = buf - start
            kc_h = kc_ref[start:buf, hs].astype(BF16)
            vc_h = vc_ref[start:buf, hs].astype(BF16)
            q_h = _stack_heads(q, gi * (c // HEAD_DIM) + h * GQA, t).astype(BF16)
            nt = (((1,), (1,)), ((), ()))
            s_c = lax.dot_general(q_h, kc_h, nt, preferred_element_type=F32)
            s_n = lax.dot_general(q_h, kn_h, nt, preferred_element_type=F32)
            tok = _mod(lax.broadcasted_iota(jnp.int32, (rows, span), 0), t)
            dist_c = buf + tok - (start + lax.broadcasted_iota(jnp.int32, (rows, span), 1))
            s_c = jnp.where((_mod(dist_c, d) == 0) & (dist_c <= window), s_c, NEG_INF)
            s_n = jnp.where((dist_n >= 0) & (_mod(dist_n, d) == 0) & (dist_n <= window), s_n, NEG_INF)
            m = jnp.maximum(jnp.max(s_c, axis=-1, keepdims=True), jnp.max(s_n, axis=-1, keepdims=True))
            e_c = jnp.exp(s_c - m)
            e_n = jnp.exp(s_n - m)
            den = jnp.sum(e_c, axis=-1, keepdims=True) + jnp.sum(e_n, axis=-1, keepdims=True)
            pv = (jnp.dot(e_c.astype(BF16), vc_h, preferred_element_type=F32)
                  + jnp.dot(e_n.astype(BF16), vn_h, preferred_element_type=F32))
            outs.append(pv * (1.0 / den))
            lses.append(jnp.log(den) + m)
        top = functools.reduce(jnp.maximum, lses)
        ws = [jnp.exp(l - top) for l in lses]
        inv = 1.0 / functools.reduce(jnp.add, ws)
        o = functools.reduce(jnp.add, [o_g * (w * inv) for o_g, w in zip(outs, ws)])
        for g in range(GQA):
            col0 = (h * GQA + g) * HEAD_DIM
            o_ref[:, col0:col0 + HEAD_DIM] = o[g * t:(g + 1) * t].astype(o_ref.dtype)


def _sample_b_attn(q, kcache, knew, vcache, vnew):
    n, t, gc = q.shape
    c = gc // len(B_GROUPS)
    buf, ck = kcache.shape[1:]
    row3 = lambda b: (b, 0, 0)
    return pl.pallas_call(
        functools.partial(_sample_b_kernel, kvh=ck // HEAD_DIM),
        grid=(n,),
        in_specs=[pl.BlockSpec((None, t, gc), row3),
                  pl.BlockSpec((None, buf, ck), row3), pl.BlockSpec((None, t, ck), row3),
                  pl.BlockSpec((None, buf, ck), row3), pl.BlockSpec((None, t, ck), row3)],
        out_specs=pl.BlockSpec((None, t, c), row3),
        out_shape=jax.ShapeDtypeStruct((n, t, c), F32),
        compiler_params=_params("parallel"),
        name="sample_b_attn",
    )(q, kcache, knew, vcache, vnew)


def _rope_tables(pos):
    half = HEAD_DIM // 2
    inv = ROPE_THETA ** (-jnp.arange(half, dtype=F32) / half)
    ang = pos.astype(F32)[:, None] * inv[None, :]
    cos, sin = jnp.cos(ang), jnp.sin(ang)
    reps = LANES // HEAD_DIM
    return jnp.tile(jnp.concatenate([cos, cos], axis=1), (1, reps)), jnp.tile(jnp.concatenate([-sin, sin], axis=1), (1, reps))


def _trunk(x, pos, sample_caches, wts):
    (ln_g, ln_b, w_qkv_a, sinks_a, w_o_a, w_kv_b, w_q_b, w_o_b, w_up, w_down) = wts
    n, t, dm = x.shape
    m = n * t
    depth = ln_g.shape[0]
    n_a = w_qkv_a.shape[0]
    alpha = (2 * depth) ** 0.25
    qa = w_o_a.shape[1]
    kva = (w_qkv_a.shape[2] - qa) // 2
    kvb = w_kv_b.shape[1] // 2
    cb = w_o_b.shape[1]
    ng = len(B_GROUPS)
    scale = HEAD_DIM ** -0.5
    is_prompt = sample_caches is None
    rope = _rope_tables(pos)
    if not is_prompt:
        rope = tuple(jnp.tile(r, (n, 1)) for r in rope)
    qdt = BF16 if is_prompt else F32

    x32 = x.reshape(m, dm)
    x16 = x32.astype(BF16)
    a_states = []
    b_state = None
    b_kv = None
    for layer in range(depth):
        if layer < n_a:
            q = _proj(x16, w_qkv_a[layer, :, :qa], qdt, rope=rope, rope_cols=qa, scale=scale)
            kv = _proj(x16, w_qkv_a[layer, :, qa:], F32, rope=rope, rope_cols=kva)
            k, v = kv[:, :kva], kv[:, kva:]
            if is_prompt:
                k3, v3 = k.reshape(n, t, kva), v.reshape(n, t, kva)
                o = _band_attn(q.reshape(n, t, qa), k3, v3, 1, A_WINDOW - 1, sinks=sinks_a[layer]).reshape(m, qa)
                keep = min(A_WINDOW, t)
                a_states.append((k3[:, -keep:], v3[:, -keep:]))
            else:
                ck = sample_caches[0][layer].reshape(n, -1, kva)
                cv = sample_caches[1][layer].reshape(n, -1, kva)
                k3, v3 = k.reshape(n, t, kva), v.reshape(n, t, kva)
                o = _sample_a_attn(q.reshape(n, t, qa), ck, k3, cv, v3, sinks_a[layer]).reshape(m, qa)
                keep = ck.shape[1]
                a_states.append((jnp.concatenate([ck, k3], axis=1)[:, -keep:],
                                 jnp.concatenate([cv, v3], axis=1)[:, -keep:]))
            x32, x16 = _res_ln(o, w_o_a[layer], x32, ln_g[layer, 0], ln_b[layer, 0], alpha)
        else:
            j = layer - n_a
            if j == 0:
                kv = _proj(x16, w_kv_b, F32, rope=rope, rope_cols=kvb)
                k3, v3 = kv[:, :kvb].reshape(n, t, kvb), kv[:, kvb:].reshape(n, t, kvb)
                if is_prompt:
                    keep = min(max(w for w, _ in B_GROUPS), t)
                    b_state = (k3[:, -keep:], v3[:, -keep:])
                    b_kv = [(k3.reshape(n, t // d, d * kvb), v3.reshape(n, t // d, d * kvb)) for _, d in B_GROUPS]
                else:
                    ck = sample_caches[2].reshape(n, -1, kvb)
                    cv = sample_caches[3].reshape(n, -1, kvb)
                    keep = ck.shape[1]
                    b_state = (jnp.concatenate([ck, k3], axis=1)[:, -keep:],
                               jnp.concatenate([cv, v3], axis=1)[:, -keep:])
                    b_kv = (ck, k3, cv, v3)
            q = _proj(x16, w_q_b[j], qdt, rope=rope, rope_cols=ng * cb, scale=scale)
            if is_prompt:
                q4 = q.reshape(n, t, ng, cb)
                outs, lses = [], []
                for gi, (window, d) in enumerate(B_GROUPS):
                    o_g, lse_g = _band_attn(q4[:, :, gi].reshape(n, t // d, d * cb), b_kv[gi][0], b_kv[gi][1],
                                            d, window // d, with_lse=True)
                    outs.append(o_g.reshape(m, cb))
                    lses.append(lse_g.reshape(m, LANES))
                o = _combine(outs, lses)
            else:
                o = _sample_b_attn(q.reshape(n, t, ng * cb), *b_kv).reshape(m, cb)
            x32, x16 = _res_ln(o, w_o_b[j], x32, ln_g[layer, 0], ln_b[layer, 0], alpha)
        hid = _proj(x16, w_up[layer], BF16, relu2=True)
        x32, x16 = _res_ln(hid, w_down[layer], x32, ln_g[layer, 1], ln_b[layer, 1], alpha)
    return x32.reshape(n, t, dm), a_states, b_state


def kernel(x_prompt, x_sample, cache_a_k, cache_a_v, cache_b_k, cache_b_v, ln_g, ln_b,
           w_qkv_a, sinks_a, w_o_a, w_kv_b, w_q_b, w_o_b, w_up, w_down):
    wts = (ln_g, ln_b, w_qkv_a.astype(BF16), sinks_a, w_o_a.astype(BF16), w_kv_b.astype(BF16),
           w_q_b.astype(BF16), w_o_b.astype(BF16), w_up.astype(BF16), w_down.astype(BF16))
    kv_heads_a = cache_a_k.shape[3]
    kv_heads_b = cache_b_k.shape[2]
    pos_prompt = jnp.arange(x_prompt.shape[1], dtype=jnp.int32)
    pos_sample = PAST_LEN + jnp.arange(x_sample.shape[1], dtype=jnp.int32)
    y_prompt, a_prompt, b_prompt = _trunk(x_prompt, pos_prompt, None, wts)
    y_sample, a_sample, b_sample = _trunk(x_sample, pos_sample, (cache_a_k, cache_a_v, cache_b_k, cache_b_v), wts)

    def heads(s, kvh):
        return s.reshape(*s.shape[:-1], kvh, HEAD_DIM)

    return (y_prompt, y_sample,
            jnp.stack([heads(s[0], kv_heads_a) for s in a_prompt], axis=0),
            jnp.stack([heads(s[1], kv_heads_a) for s in a_prompt], axis=0),
            jnp.stack([heads(s[0], kv_heads_a) for s in a_sample], axis=0),
            jnp.stack([heads(s[1], kv_heads_a) for s in a_sample], axis=0),
            heads(b_prompt[0], kv_heads_b), heads(b_prompt[1], kv_heads_b),
            heads(b_sample[0], kv_heads_b), heads(b_sample[1], kv_heads_b))
```

```python
import functools

import jax
import jax.numpy as jnp
from jax import lax
from jax.experimental import pallas as pl
from jax.experimental.pallas import tpu as pltpu

HEAD_DIM = 64
GQA = 8
A_WINDOW = 128
B_GROUPS = ((128, 1), (512, 4), (2048, 16))
PAST_LEN = 16384
ROPE_THETA = 10000.0
LN_EPS = 1e-5
BLOCK = 128
NEG_INF = -1e30
LANES = 128
VMEM_LIMIT = 56 * 1024 * 1024

LN_ROWS = 16

BF16 = jnp.bfloat16
F32 = jnp.float32


def _mod(x, n):
    assert n & (n - 1) == 0, "power-of-two modulus only"
    return x & (n - 1)


def _div(x, n):
    assert n & (n - 1) == 0, "power-of-two divisor only"
    return x >> (n.bit_length() - 1)


def _params(*sem):
    return pltpu.CompilerParams(dimension_semantics=sem, vmem_limit_bytes=VMEM_LIMIT)


def _pick(n, pref):
    if n <= pref:
        return n
    t = pref
    while n % t:
        t //= 2
    return t


def _rope_chunk(x, cos, sin_signed):
    lane = lax.broadcasted_iota(jnp.int32, x.shape, 1)
    first = _mod(lane, HEAD_DIM) < (HEAD_DIM // 2)
    partner = jnp.where(first, pltpu.roll(x, LANES - HEAD_DIM // 2, 1), pltpu.roll(x, HEAD_DIM // 2, 1))
    return x * cos + partner * sin_signed


def _proj_kernel(*refs, rope_chunks, scale, relu2):
    if rope_chunks:
        a_ref, w_ref, cos_ref, sin_ref, o_ref = refs
    else:
        a_ref, w_ref, o_ref = refs
    acc = jnp.dot(a_ref[...].astype(BF16), w_ref[...], preferred_element_type=F32)
    if relu2:
        acc = jnp.square(jnp.maximum(acc, 0.0))
    if not rope_chunks:
        o_ref[...] = acc.astype(o_ref.dtype)
        return
    cos = cos_ref[...]
    sin = sin_ref[...]
    for c in range(acc.shape[1] // LANES):
        x = acc[:, c * LANES:(c + 1) * LANES]
        if c < rope_chunks:
            x = _rope_chunk(x, cos, sin)
            if scale != 1.0:
                x = x * scale
        o_ref[:, c * LANES:(c + 1) * LANES] = x.astype(o_ref.dtype)


def _proj(a, w, out_dtype, *, rope=None, rope_cols=0, scale=1.0, relu2=False, tm=1024, tn=1024):
    m, k = a.shape
    n = w.shape[1]
    tm = _pick(m, tm)
    tn = _pick(n, tn)
    in_specs = [pl.BlockSpec((tm, k), lambda j, i: (i, 0)), pl.BlockSpec((k, tn), lambda j, i: (0, j))]
    args = [a, w]
    rope_chunks = 0
    if rope is not None:
        cos, sin = rope
        p = cos.shape[0]
        tm = _pick(p, tm)
        pb = p // tm
        in_specs[0] = pl.BlockSpec((tm, k), lambda j, i: (i, 0))
        in_specs += [pl.BlockSpec((tm, LANES), lambda j, i: (i % pb, 0))] * 2
        args += [cos, sin]
        assert rope_cols == n or tn == n
        rope_chunks = min(rope_cols, tn) // LANES
    return pl.pallas_call(
        functools.partial(_proj_kernel, rope_chunks=rope_chunks, scale=scale, relu2=relu2),
        grid=(n // tn, m // tm),
        in_specs=in_specs,
        out_specs=pl.BlockSpec((tm, tn), lambda j, i: (i, j)),
        out_shape=jax.ShapeDtypeStruct((m, n), out_dtype),
        compiler_params=_params("parallel", "arbitrary"),
        name="proj",
    )(*args)


def _res_ln_kernel(a_ref, w_ref, x_ref, g_ref, b_ref, o32_ref, o16_ref, acc0_ref, acc1_ref, *, alpha, mt, nk):
    i = pl.program_id(0)
    kk = pl.program_id(1)
    accs = (acc0_ref, acc1_ref)
    rc = acc0_ref.shape[0] // nk

    def matmul(acc_ref):
        acc_ref[...] += jnp.dot(a_ref[...].astype(BF16), w_ref[...], preferred_element_type=F32)

    def layernorm(acc_ref):
        gain = g_ref[...]
        bias = b_ref[...]
        for s in range(rc // LN_ROWS):
            rows = pl.ds(pl.multiple_of(kk * rc + s * LN_ROWS, LN_ROWS), LN_ROWS)
            v = alpha * x_ref[rows, :] + acc_ref[rows, :]
            acc_ref[rows, :] = jnp.zeros((LN_ROWS, acc_ref.shape[1]), F32)
            mu = jnp.mean(v, axis=-1, keepdims=True)
            c = v - mu
            var = jnp.mean(c * c, axis=-1, keepdims=True)
            y = c * lax.rsqrt(var + LN_EPS) * gain + bias
            o32_ref[rows, :] = y
            o16_ref[rows, :] = y.astype(BF16)

    @pl.when((i == 0) & (kk == 0))
    def _():
        acc0_ref[...] = jnp.zeros_like(acc0_ref)
        acc1_ref[...] = jnp.zeros_like(acc1_ref)

    @pl.when(i == 0)
    def _():
        matmul(acc0_ref)

    for parity in range(2):
        @pl.when((i > 0) & (i < mt) & (i % 2 == parity))
        def _():
            matmul(accs[parity])
            layernorm(accs[1 - parity])

    @pl.when(i == mt)
    def _():
        layernorm(accs[(mt - 1) % 2])


def _res_ln(a, w, x, g, b, alpha, *, tm=512, tk=1024):
    m, k = a.shape
    n = w.shape[1]
    tm = _pick(m, tm)
    tk = _pick(k, tk)
    while tk < k and (tm // (k // tk)) % LN_ROWS:
        tk *= 2
    mt, nk = m // tm, k // tk
    assert (tm // nk) % LN_ROWS == 0 and tm % nk == 0
    a_map = lambda i, kk: (jnp.minimum(i, mt - 1), jnp.where(i < mt, kk, nk - 1))
    w_map = lambda i, kk: (jnp.where(i < mt, kk, nk - 1), 0)
    prev_map = lambda i, kk: (jnp.maximum(i - 1, 0), 0)
    return pl.pallas_call(
        functools.partial(_res_ln_kernel, alpha=alpha, mt=mt, nk=nk),
        grid=(mt + 1, nk),
        in_specs=[
            pl.BlockSpec((tm, tk), a_map),
            pl.BlockSpec((tk, n), w_map),
            pl.BlockSpec((tm, n), prev_map),
            pl.BlockSpec((1, n), lambda i, kk: (0, 0)),
            pl.BlockSpec((1, n), lambda i, kk: (0, 0)),
        ],
        out_specs=[pl.BlockSpec((tm, n), prev_map), pl.BlockSpec((tm, n), prev_map)],
        out_shape=[jax.ShapeDtypeStruct((m, n), F32), jax.ShapeDtypeStruct((m, n), BF16)],
        scratch_shapes=[pltpu.VMEM((tm, n), F32), pltpu.VMEM((tm, n), F32)],
        compiler_params=_params("arbitrary", "arbitrary"),
        name="res_ln",
    )(a, w, x, g.reshape(1, n), b.reshape(1, n))


def _band_kernel(*refs, n_back, kvh, has_sink, with_lse):
    refs = list(refs)
    sink_ref = refs.pop(0) if has_sink else None
    q_ref, kp_ref, kc_ref, vp_ref, vc_ref, o_ref = refs[:6]
    lse_ref = refs[6] if with_lse else None
    i = pl.program_id(1)
    q = q_ref[...]
    kk = jnp.concatenate([kp_ref[...], kc_ref[...]], axis=0).astype(BF16)
    vv = jnp.concatenate([vp_ref[...], vc_ref[...]], axis=0).astype(BF16)
    qi = lax.broadcasted_iota(jnp.int32, (BLOCK, 2 * BLOCK), 0)
    ki = lax.broadcasted_iota(jnp.int32, (BLOCK, 2 * BLOCK), 1)
    dist = BLOCK + qi - ki
    valid = (dist >= 0) & (dist <= n_back) & ((ki >= BLOCK) | (i > 0))
    if with_lse:
        lane = lax.broadcasted_iota(jnp.int32, (BLOCK, LANES), 1)
        lse_tile = jnp.zeros((BLOCK, LANES), F32)
    for h in range(kvh):
        k_h = kk[:, h * HEAD_DIM:(h + 1) * HEAD_DIM]
        v_h = vv[:, h * HEAD_DIM:(h + 1) * HEAD_DIM]
        q_h = jnp.concatenate(
            [q[:, (h * GQA + g) * HEAD_DIM:(h * GQA + g + 1) * HEAD_DIM] for g in range(GQA)], axis=0)
        s = lax.dot_general(q_h, k_h, (((1,), (1,)), ((), ())), preferred_element_type=F32)
        es, invs = [], []
        for g in range(GQA):
            s_g = jnp.where(valid, s[g * BLOCK:(g + 1) * BLOCK], NEG_INF)
            m = jnp.max(s_g, axis=-1, keepdims=True)
            if has_sink:
                sink = sink_ref[h * GQA + g]
                m = jnp.maximum(m, sink)
            e = jnp.exp(s_g - m)
            den = jnp.sum(e, axis=-1, keepdims=True)
            if has_sink:
                den = den + jnp.exp(sink - m)
            es.append(e.astype(BF16))
            invs.append(1.0 / den)
            if with_lse:
                lse_tile = jnp.where(lane == h * GQA + g, jnp.log(den) + m, lse_tile)
        pv = jnp.dot(jnp.concatenate(es, axis=0), v_h, preferred_element_type=F32)
        for g in range(0, GQA, 2):
            pair = jnp.concatenate([pv[g * BLOCK:(g + 1) * BLOCK] * invs[g],
                                    pv[(g + 1) * BLOCK:(g + 2) * BLOCK] * invs[g + 1]], axis=1)
            col = (h * GQA + g) * HEAD_DIM
            o_ref[:, col:col + 2 * HEAD_DIM] = pair.astype(o_ref.dtype)
    if with_lse:
        lse_ref[...] = lse_tile


def _band_attn(q, k, v, n_back, sinks=None, with_lse=False):
    n, length, c = q.shape
    ck = k.shape[2]
    kvh = ck // HEAD_DIM
    assert c == kvh * GQA * HEAD_DIM and length % BLOCK == 0 and n_back <= BLOCK
    nb = length // BLOCK
    cur = lambda b, i: (b, i, 0)
    prev = lambda b, i: (b, jnp.maximum(i - 1, 0), 0)
    in_specs = [
        pl.BlockSpec((None, BLOCK, c), cur),
        pl.BlockSpec((None, BLOCK, ck), prev), pl.BlockSpec((None, BLOCK, ck), cur),
        pl.BlockSpec((None, BLOCK, ck), prev), pl.BlockSpec((None, BLOCK, ck), cur),
    ]
    args = [q, k, k, v, v]
    if sinks is not None:
        in_specs = [pl.BlockSpec(memory_space=pltpu.SMEM)] + in_specs
        args = [sinks.astype(F32)] + args
    out_specs = [pl.BlockSpec((None, BLOCK, c), cur)]
    out_shape = [jax.ShapeDtypeStruct(q.shape, BF16)]
    if with_lse:
        out_specs.append(pl.BlockSpec((None, BLOCK, LANES), cur))
        out_shape.append(jax.ShapeDtypeStruct((n, length, LANES), F32))
    res = pl.pallas_call(
        functools.partial(_band_kernel, n_back=n_back, kvh=kvh, has_sink=sinks is not None, with_lse=with_lse),
        grid=(n, nb),
        in_specs=in_specs,
        out_specs=out_specs,
        out_shape=out_shape,
        compiler_params=_params("parallel", "arbitrary"),
        name="band_attn",
    )(*args)
    return res if with_lse else res[0]


def _permute_kernel(x_ref, o_ref):
    d = o_ref.shape[0]
    rows = o_ref.shape[1]
    for r in range(d):
        o_ref[r] = x_ref[pl.ds(r, rows, stride=d), :].astype(o_ref.dtype)


def _by_residue(x, d, out_dtype):
    n, t, c = x.shape
    return pl.pallas_call(
        _permute_kernel,
        grid=(n, c // LANES),
        in_specs=[pl.BlockSpec((None, t, LANES), lambda b, j: (b, 0, j))],
        out_specs=pl.BlockSpec((None, d, t // d, LANES), lambda b, j: (b, 0, 0, j)),
        out_shape=jax.ShapeDtypeStruct((n, d, t // d, c), out_dtype),
        compiler_params=_params("parallel", "parallel"),
        name="by_residue",
    )(x)


def _combine_kernel(*refs, dils):
    ng = len(dils)
    o_refs, l_refs, out_ref = refs[:ng], refs[ng:2 * ng], refs[2 * ng]
    stage = list(refs[2 * ng + 1:])
    c = out_ref.shape[1]
    nchunk = c // LANES
    outs, lses = [], []
    for o_ref, l_ref, d in zip(o_refs, l_refs, dils):
        if d == 1:
            outs.append(lambda j, o_ref=o_ref: o_ref[0, :, j * LANES:(j + 1) * LANES].astype(F32))
            lses.append(l_ref[0])
            continue
        o_st, l_st = stage.pop(0), stage.pop(0)
        rows = o_ref.shape[1]
        for r in range(d):
            l_st[pl.ds(r, rows, stride=d), :] = l_ref[r]
            for j in range(nchunk):
                o_st[j, pl.ds(r, rows, stride=d), :] = o_ref[r, :, j * LANES:(j + 1) * LANES].astype(F32)
        outs.append(lambda j, o_st=o_st: o_st[j])
        lses.append(l_st[...])
    top = functools.reduce(jnp.maximum, lses)
    ws = [jnp.exp(l - top) for l in lses]
    inv = 1.0 / functools.reduce(jnp.add, ws)
    his, los = [], []
    for w in ws:
        w = w * inv
        his.append(w.astype(BF16))
        los.append((w - his[-1].astype(F32)).astype(BF16))
    for j in range(nchunk):
        head_of_col = _div(lax.broadcasted_iota(jnp.int32, (LANES, LANES), 1) + j * LANES, HEAD_DIM)
        expand = (head_of_col == lax.broadcasted_iota(jnp.int32, (LANES, LANES), 0)).astype(BF16)
        acc = None
        for o, hi, lo in zip(outs, his, los):
            wide = (jnp.dot(hi, expand, preferred_element_type=F32)
                    + jnp.dot(lo, expand, preferred_element_type=F32))
            term = wide * o(j)
            acc = term if acc is None else acc + term
        out_ref[:, j * LANES:(j + 1) * LANES] = acc.astype(out_ref.dtype)


def _combine(outs, lses, dils, tm=256):
    n, _, _, c = outs[0].shape
    t = outs[0].shape[1] * outs[0].shape[2]
    tm = _pick(t, tm)
    assert all(tm % (16 * d) == 0 for d in dils)
    blk = lambda width: [pl.BlockSpec((None, d, tm // d, width), lambda b, i: (b, 0, i, 0)) for d in dils]
    scratch = []
    for d in dils:
        if d > 1:
            scratch += [pltpu.VMEM((c // LANES, tm, LANES), F32), pltpu.VMEM((tm, LANES), F32)]
    return pl.pallas_call(
        functools.partial(_combine_kernel, dils=tuple(dils)),
        grid=(n, t // tm),
        in_specs=blk(c) + blk(LANES),
        out_specs=pl.BlockSpec((None, tm, c), lambda b, i: (b, i, 0)),
        out_shape=jax.ShapeDtypeStruct((n, t, c), BF16),
        scratch_shapes=scratch,
        compiler_params=_params("parallel", "parallel"),
        name="combine",
    )(*outs, *lses)


def _stack_heads(q, base, t):
    return jnp.concatenate([q[:, (base + g) * HEAD_DIM:(base + g + 1) * HEAD_DIM] for g in range(GQA)], axis=0)


def _sample_a_kernel(sink_ref, q_ref, kc_ref, kn_ref, vc_ref, vn_ref, o_ref, *, kvh, window):
    t = q_ref.shape[0]
    buf = kc_ref.shape[0]
    q = q_ref[...]
    kc = jnp.concatenate([kc_ref[...], kn_ref[...]], axis=0).astype(BF16)
    vc = jnp.concatenate([vc_ref[...], vn_ref[...]], axis=0).astype(BF16)
    rows = GQA * t
    row = lax.broadcasted_iota(jnp.int32, (rows, buf + t), 0)
    col = lax.broadcasted_iota(jnp.int32, (rows, buf + t), 1)
    dist = buf + _mod(row, t) - col
    valid = (dist >= 0) & (dist < window)
    rowg = _div(lax.broadcasted_iota(jnp.int32, (rows, 1), 0), t)
    for h in range(kvh):
        k_h = kc[:, h * HEAD_DIM:(h + 1) * HEAD_DIM]
        v_h = vc[:, h * HEAD_DIM:(h + 1) * HEAD_DIM]
        q_h = _stack_heads(q, h * GQA, t).astype(BF16)
        s = lax.dot_general(q_h, k_h, (((1,), (1,)), ((), ())), preferred_element_type=F32)
        s = jnp.where(valid, s, NEG_INF)
        sink = jnp.zeros((rows, 1), F32)
        for g in range(GQA):
            sink = jnp.where(rowg == g, sink_ref[h * GQA + g], sink)
        m = jnp.maximum(jnp.max(s, axis=-1, keepdims=True), sink)
        e = jnp.exp(s - m)
        den = jnp.sum(e, axis=-1, keepdims=True) + jnp.exp(sink - m)
        o = jnp.dot(e.astype(BF16), v_h, preferred_element_type=F32) * (1.0 / den)
        for g in range(GQA):
            col0 = (h * GQA + g) * HEAD_DIM
            o_ref[:, col0:col0 + HEAD_DIM] = o[g * t:(g + 1) * t].astype(o_ref.dtype)


def _sample_a_attn(q, kcache, knew, vcache, vnew, sinks):
    n, t, c = q.shape
    buf, ck = kcache.shape[1:]
    row3 = lambda b: (b, 0, 0)
    return pl.pallas_call(
        functools.partial(_sample_a_kernel, kvh=ck // HEAD_DIM, window=A_WINDOW),
        grid=(n,),
        in_specs=[pl.BlockSpec(memory_space=pltpu.SMEM),
                  pl.BlockSpec((None, t, c), row3),
                  pl.BlockSpec((None, buf, ck), row3), pl.BlockSpec((None, t, ck), row3),
                  pl.BlockSpec((None, buf, ck), row3), pl.BlockSpec((None, t, ck), row3)],
        out_specs=pl.BlockSpec((None, t, c), row3),
        out_shape=jax.ShapeDtypeStruct((n, t, c), F32),
        compiler_params=_params("parallel"),
        name="sample_a_attn",
    )(sinks.astype(F32), q, kcache, knew, vcache, vnew)


def _sample_b_kernel(q_ref, kc_ref, kn_ref, vc_ref, vn_ref, o_ref, *, kvh):
    t = q_ref.shape[0]
    buf = kc_ref.shape[0]
    c = o_ref.shape[1]
    rows = GQA * t
    q = q_ref[...]
    kn = kn_ref[...].astype(BF16)
    vn = vn_ref[...].astype(BF16)
    tok_n = _mod(lax.broadcasted_iota(jnp.int32, (rows, t), 0), t)
    dist_n = tok_n - lax.broadcasted_iota(jnp.int32, (rows, t), 1)
    for h in range(kvh):
        hs = slice(h * HEAD_DIM, (h + 1) * HEAD_DIM)
        kn_h, vn_h = kn[:, hs], vn[:, hs]
        outs, lses = [], []
        for gi, (window, d) in enumerate(B_GROUPS):
            start = max(buf - window, 0)
            span = buf - start
            kc_h = kc_ref[start:buf, hs].astype(BF16)
            vc_h = vc_ref[start:buf, hs].astype(BF16)
            q_h = _stack_heads(q, gi * (c // HEAD_DIM) + h * GQA, t).astype(BF16)
            nt = (((1,), (1,)), ((), ()))
            s_c = lax.dot_general(q_h, kc_h, nt, preferred_element_type=F32)
            s_n = lax.dot_general(q_h, kn_h, nt, preferred_element_type=F32)
            tok = _mod(lax.broadcasted_iota(jnp.int32, (rows, span), 0), t)
            dist_c = buf + tok - (start + lax.broadcasted_iota(jnp.int32, (rows, span), 1))
            s_c = jnp.where((_mod(dist_c, d) == 0) & (dist_c <= window), s_c, NEG_INF)
            s_n = jnp.where((dist_n >= 0) & (_mod(dist_n, d) == 0) & (dist_n <= window), s_n, NEG_INF)
            m = jnp.maximum(jnp.max(s_c, axis=-1, keepdims=True), jnp.max(s_n, axis=-1, keepdims=True))
            e_c = jnp.exp(s_c - m)
            e_n = jnp.exp(s_n - m)
            den = jnp.sum(e_c, axis=-1, keepdims=True) + jnp.sum(e_n, axis=-1, keepdims=True)
            pv = (jnp.dot(e_c.astype(BF16), vc_h, preferred_element_type=F32)
                  + jnp.dot(e_n.astype(BF16), vn_h, preferred_element_type=F32))
            outs.append(pv * (1.0 / den))
            lses.append(jnp.log(den) + m)
        top = functools.reduce(jnp.maximum, lses)
        ws = [jnp.exp(l - top) for l in lses]
        inv = 1.0 / functools.reduce(jnp.add, ws)
        o = functools.reduce(jnp.add, [o_g * (w * inv) for o_g, w in zip(outs, ws)])
        for g in range(GQA):
            col0 = (h * GQA + g) * HEAD_DIM
            o_ref[:, col0:col0 + HEAD_DIM] = o[g * t:(g + 1) * t].astype(o_ref.dtype)


def _sample_b_attn(q, kcache, knew, vcache, vnew):
    n, t, gc = q.shape
    c = gc // len(B_GROUPS)
    buf, ck = kcache.shape[1:]
    row3 = lambda b: (b, 0, 0)
    return pl.pallas_call(
        functools.partial(_sample_b_kernel, kvh=ck // HEAD_DIM),
        grid=(n,),
        in_specs=[pl.BlockSpec((None, t, gc), row3),
                  pl.BlockSpec((None, buf, ck), row3), pl.BlockSpec((None, t, ck), row3),
                  pl.BlockSpec((None, buf, ck), row3), pl.BlockSpec((None, t, ck), row3)],
        out_specs=pl.BlockSpec((None, t, c), row3),
        out_shape=jax.ShapeDtypeStruct((n, t, c), F32),
        compiler_params=_params("parallel"),
        name="sample_b_attn",
    )(q, kcache, knew, vcache, vnew)


def _rope_tables(pos):
    half = HEAD_DIM // 2
    inv = ROPE_THETA ** (-jnp.arange(half, dtype=F32) / half)
    ang = pos.astype(F32)[:, None] * inv[None, :]
    cos, sin = jnp.cos(ang), jnp.sin(ang)
    reps = LANES // HEAD_DIM
    return jnp.tile(jnp.concatenate([cos, cos], axis=1), (1, reps)), jnp.tile(jnp.concatenate([-sin, sin], axis=1), (1, reps))


def _trunk(x, pos, sample_caches, wts):
    (ln_g, ln_b, w_qkv_a, sinks_a, w_o_a, w_kv_b, w_q_b, w_o_b, w_up, w_down) = wts
    n, t, dm = x.shape
    m = n * t
    depth = ln_g.shape[0]
    n_a = w_qkv_a.shape[0]
    alpha = (2 * depth) ** 0.25
    qa = w_o_a.shape[1]
    kva = (w_qkv_a.shape[2] - qa) // 2
    kvb = w_kv_b.shape[1] // 2
    cb = w_o_b.shape[1]
    ng = len(B_GROUPS)
    scale = HEAD_DIM ** -0.5
    is_prompt = sample_caches is None
    rope = _rope_tables(pos)
    if is_prompt:
        rope_by_d = {d: rope if d == 1 else _rope_tables(pos.reshape(t // d, d).T.reshape(t)) for _, d in B_GROUPS}
    if not is_prompt:
        rope = tuple(jnp.tile(r, (n, 1)) for r in rope)
    qdt = BF16 if is_prompt else F32

    x32 = x.reshape(m, dm)
    x16 = x32.astype(BF16)
    a_states = []
    b_state = None
    b_kv = None
    for layer in range(depth):
        if layer < n_a:
            q = _proj(x16, w_qkv_a[layer, :, :qa], qdt, rope=rope, rope_cols=qa, scale=scale)
            kv = _proj(x16, w_qkv_a[layer, :, qa:], F32, rope=rope, rope_cols=kva)
            k, v = kv[:, :kva], kv[:, kva:]
            if is_prompt:
                k3, v3 = k.reshape(n, t, kva), v.reshape(n, t, kva)
                o = _band_attn(q.reshape(n, t, qa), k3, v3, A_WINDOW - 1, sinks=sinks_a[layer]).reshape(m, qa)
                keep = min(A_WINDOW, t)
                a_states.append((k3[:, -keep:], v3[:, -keep:]))
            else:
                ck = sample_caches[0][layer].reshape(n, -1, kva)
                cv = sample_caches[1][layer].reshape(n, -1, kva)
                k3, v3 = k.reshape(n, t, kva), v.reshape(n, t, kva)
                o = _sample_a_attn(q.reshape(n, t, qa), ck, k3, cv, v3, sinks_a[layer]).reshape(m, qa)
                keep = ck.shape[1]
                a_states.append((jnp.concatenate([ck, k3], axis=1)[:, -keep:],
                                 jnp.concatenate([cv, v3], axis=1)[:, -keep:]))
            x32, x16 = _res_ln(o, w_o_a[layer], x32, ln_g[layer, 0], ln_b[layer, 0], alpha)
        else:
            j = layer - n_a
            if j == 0:
                kv = _proj(x16, w_kv_b, F32, rope=rope, rope_cols=kvb)
                k3, v3 = kv[:, :kvb].reshape(n, t, kvb), kv[:, kvb:].reshape(n, t, kvb)
                if is_prompt:
                    keep = min(max(w for w, _ in B_GROUPS), t)
                    b_state = (k3[:, -keep:], v3[:, -keep:])
                    b_kv = [(k3, v3) if d == 1 else
                            tuple(_by_residue(a, d, F32).reshape(n * d, t // d, kvb) for a in (k3, v3))
                            for _, d in B_GROUPS]
                else:
                    ck = sample_caches[2].reshape(n, -1, kvb)
                    cv = sample_caches[3].reshape(n, -1, kvb)
                    keep = ck.shape[1]
                    b_state = (jnp.concatenate([ck, k3], axis=1)[:, -keep:],
                               jnp.concatenate([cv, v3], axis=1)[:, -keep:])
                    b_kv = (ck, k3, cv, v3)
            if is_prompt:
                outs, lses = [], []
                for gi, (window, d) in enumerate(B_GROUPS):
                    xg = x16 if d == 1 else _by_residue(x32.reshape(n, t, dm), d, BF16).reshape(m, dm)
                    q = _proj(xg, w_q_b[j, :, gi * cb:(gi + 1) * cb], BF16, rope=rope_by_d[d], rope_cols=cb, scale=scale)
                    o_g, lse_g = _band_attn(q.reshape(n * d, t // d, cb), b_kv[gi][0], b_kv[gi][1],
                                            window // d, with_lse=True)
                    outs.append(o_g.reshape(n, d, t // d, cb))
                    lses.append(lse_g.reshape(n, d, t // d, LANES))
                o = _combine(outs, lses, [d for _, d in B_GROUPS]).reshape(m, cb)
            else:
                q = _proj(x16, w_q_b[j], qdt, rope=rope, rope_cols=ng * cb, scale=scale)
                o = _sample_b_attn(q.reshape(n, t, ng * cb), *b_kv).reshape(m, cb)
            x32, x16 = _res_ln(o, w_o_b[j], x32, ln_g[layer, 0], ln_b[layer, 0], alpha)
        hid = _proj(x16, w_up[layer], BF16, relu2=True)
        x32, x16 = _res_ln(hid, w_down[layer], x32, ln_g[layer, 1], ln_b[layer, 1], alpha)
    return x32.reshape(n, t, dm), a_states, b_state


def kernel(x_prompt, x_sample, cache_a_k, cache_a_v, cache_b_k, cache_b_v, ln_g, ln_b,
           w_qkv_a, sinks_a, w_o_a, w_kv_b, w_q_b, w_o_b, w_up, w_down):
    wts = (ln_g, ln_b, w_qkv_a.astype(BF16), sinks_a, w_o_a.astype(BF16), w_kv_b.astype(BF16),
           w_q_b.astype(BF16), w_o_b.astype(BF16), w_up.astype(BF16), w_down.astype(BF16))
    kv_heads_a = cache_a_k.shape[3]
    kv_heads_b = cache_b_k.shape[2]
    pos_prompt = jnp.arange(x_prompt.shape[1], dtype=jnp.int32)
    pos_sample = PAST_LEN + jnp.arange(x_sample.shape[1], dtype=jnp.int32)
    y_prompt, a_prompt, b_prompt = _trunk(x_prompt, pos_prompt, None, wts)
    y_sample, a_sample, b_sample = _trunk(x_sample, pos_sample, (cache_a_k, cache_a_v, cache_b_k, cache_b_v), wts)

    def heads(s, kvh):
        return s.reshape(*s.shape[:-1], kvh, HEAD_DIM)

    return (y_prompt, y_sample,
            jnp.stack([heads(s[0], kv_heads_a) for s in a_prompt], axis=0),
            jnp.stack([heads(s[1], kv_heads_a) for s in a_prompt], axis=0),
            jnp.stack([heads(s[0], kv_heads_a) for s in a_sample], axis=0),
            jnp.stack([heads(s[1], kv_heads_a) for s in a_sample], axis=0),
            heads(b_prompt[0], kv_heads_b), heads(b_prompt[1], kv_heads_b),
            heads(b_sample[0], kv_heads_b), heads(b_sample[1], kv_heads_b))
```

```python
import functools

import jax
import jax.numpy as jnp
from jax import lax
from jax.experimental import pallas as pl
from jax.experimental.pallas import tpu as pltpu

HEAD_DIM = 64
GQA = 8
A_WINDOW = 128
B_GROUPS = ((128, 1), (512, 4), (2048, 16))
PAST_LEN = 16384
ROPE_THETA = 10000.0
LN_EPS = 1e-5
BLOCK = 128
NEG_INF = -1e30
LANES = 128
VMEM_LIMIT = 56 * 1024 * 1024

LN_ROWS = 16

BF16 = jnp.bfloat16
F32 = jnp.float32


def _mod(x, n):
    assert n & (n - 1) == 0, "power-of-two modulus only"
    return x & (n - 1)


def _div(x, n):
    assert n & (n - 1) == 0, "power-of-two divisor only"
    return x >> (n.bit_length() - 1)


def _params(*sem):
    return pltpu.CompilerParams(dimension_semantics=sem, vmem_limit_bytes=VMEM_LIMIT)


def _pick(n, pref):
    if n <= pref:
        return n
    t = pref
    while n % t:
        t //= 2
    return t


def _rope_chunk(x, cos, sin_signed):
    lane = lax.broadcasted_iota(jnp.int32, x.shape, 1)
    first = _mod(lane, HEAD_DIM) < (HEAD_DIM // 2)
    partner = jnp.where(first, pltpu.roll(x, LANES - HEAD_DIM // 2, 1), pltpu.roll(x, HEAD_DIM // 2, 1))
    return x * cos + partner * sin_signed


def _proj_kernel(*refs, rope_chunks, scale, relu2, cast_w):
    refs = list(refs)
    a_ref, w_ref = refs[:2]
    cos_ref, sin_ref = refs[2:4] if rope_chunks else (None, None)
    o_ref = refs[4 if rope_chunks else 2]
    if cast_w:
        w16_out_ref, w16_ref = refs[-2:]

        @pl.when(pl.program_id(1) == 0)
        def _():
            w16 = w_ref[...].astype(BF16)
            w16_ref[...] = w16
            w16_out_ref[...] = w16

        w = w16_ref[...]
    else:
        w = w_ref[...]
    acc = jnp.dot(a_ref[...].astype(BF16), w, preferred_element_type=F32)
    if relu2:
        acc = jnp.square(jnp.maximum(acc, 0.0))
    if not rope_chunks:
        o_ref[...] = acc.astype(o_ref.dtype)
        return
    cos = cos_ref[...]
    sin = sin_ref[...]
    for c in range(acc.shape[1] // LANES):
        x = acc[:, c * LANES:(c + 1) * LANES]
        if c < rope_chunks:
            x = _rope_chunk(x, cos, sin)
            if scale != 1.0:
                x = x * scale
        o_ref[:, c * LANES:(c + 1) * LANES] = x.astype(o_ref.dtype)


def _proj(a, w, out_dtype, *, col0=0, n=None, rope=None, rope_cols=0, scale=1.0, relu2=False, tm=1024, tn=1024):
    m, k = a.shape
    n = w.shape[1] if n is None else n
    tm = _pick(m, tm)
    tn = _pick(n, tn)
    assert col0 % tn == 0
    joff = col0 // tn
    cast_w = w.dtype != BF16
    rope_chunks = 0
    if rope is not None:
        cos, sin = rope
        p = cos.shape[0]
        tm = _pick(p, tm)
        pb = p // tm
        assert rope_cols == n or tn == n
        rope_chunks = min(rope_cols, tn) // LANES
    in_specs = [pl.BlockSpec((tm, k), lambda j, i: (i, 0)), pl.BlockSpec((k, tn), lambda j, i: (0, j + joff))]
    args = [a, w]
    if rope is not None:
        in_specs += [pl.BlockSpec((tm, LANES), lambda j, i: (i % pb, 0))] * 2
        args += [cos, sin]
    out_specs = [pl.BlockSpec((tm, tn), lambda j, i: (i, j))]
    out_shape = [jax.ShapeDtypeStruct((m, n), out_dtype)]
    scratch = []
    if cast_w:
        out_specs.append(pl.BlockSpec((k, tn), lambda j, i: (0, j)))
        out_shape.append(jax.ShapeDtypeStruct((k, n), BF16))
        scratch.append(pltpu.VMEM((k, tn), BF16))
    res = pl.pallas_call(
        functools.partial(_proj_kernel, rope_chunks=rope_chunks, scale=scale, relu2=relu2, cast_w=cast_w),
        grid=(n // tn, m // tm),
        in_specs=in_specs,
        out_specs=out_specs,
        out_shape=out_shape,
        scratch_shapes=scratch,
        compiler_params=_params("parallel", "arbitrary"),
        name="proj",
    )(*args)
    return tuple(res) if cast_w else res[0]


def _res_ln_kernel(a_ref, w_ref, x_ref, g_ref, b_ref, o32_ref, o16_ref, acc0_ref, acc1_ref, *, alpha, mt, nk):
    i = pl.program_id(0)
    kk = pl.program_id(1)
    accs = (acc0_ref, acc1_ref)
    rc = acc0_ref.shape[0] // nk

    def matmul(acc_ref):
        acc_ref[...] += jnp.dot(a_ref[...].astype(BF16), w_ref[...], preferred_element_type=F32)

    def layernorm(acc_ref):
        gain = g_ref[...]
        bias = b_ref[...]
        for s in range(rc // LN_ROWS):
            rows = pl.ds(pl.multiple_of(kk * rc + s * LN_ROWS, LN_ROWS), LN_ROWS)
            v = alpha * x_ref[rows, :] + acc_ref[rows, :]
            acc_ref[rows, :] = jnp.zeros((LN_ROWS, acc_ref.shape[1]), F32)
            mu = jnp.mean(v, axis=-1, keepdims=True)
            c = v - mu
            var = jnp.mean(c * c, axis=-1, keepdims=True)
            y = c * lax.rsqrt(var + LN_EPS) * gain + bias
            o32_ref[rows, :] = y
            o16_ref[rows, :] = y.astype(BF16)

    @pl.when((i == 0) & (kk == 0))
    def _():
        acc0_ref[...] = jnp.zeros_like(acc0_ref)
        acc1_ref[...] = jnp.zeros_like(acc1_ref)

    @pl.when(i == 0)
    def _():
        matmul(acc0_ref)

    for parity in range(2):
        @pl.when((i > 0) & (i < mt) & (i % 2 == parity))
        def _():
            matmul(accs[parity])
            layernorm(accs[1 - parity])

    @pl.when(i == mt)
    def _():
        layernorm(accs[(mt - 1) % 2])


def _res_ln(a, w, x, g, b, alpha, *, tm=512, tk=2048):
    m, k = a.shape
    n = w.shape[1]
    tm = _pick(m, tm)
    tk = _pick(k, tk)
    while tk < k and (tm // (k // tk)) % LN_ROWS:
        tk *= 2
    mt, nk = m // tm, k // tk
    assert (tm // nk) % LN_ROWS == 0 and tm % nk == 0
    a_map = lambda i, kk: (jnp.minimum(i, mt - 1), jnp.where(i < mt, kk, nk - 1))
    w_map = lambda i, kk: (jnp.where(i < mt, kk, nk - 1), 0)
    prev_map = lambda i, kk: (jnp.maximum(i - 1, 0), 0)
    return pl.pallas_call(
        functools.partial(_res_ln_kernel, alpha=alpha, mt=mt, nk=nk),
        grid=(mt + 1, nk),
        in_specs=[
            pl.BlockSpec((tm, tk), a_map),
            pl.BlockSpec((tk, n), w_map),
            pl.BlockSpec((tm, n), prev_map),
            pl.BlockSpec((1, n), lambda i, kk: (0, 0)),
            pl.BlockSpec((1, n), lambda i, kk: (0, 0)),
        ],
        out_specs=[pl.BlockSpec((tm, n), prev_map), pl.BlockSpec((tm, n), prev_map)],
        out_shape=[jax.ShapeDtypeStruct((m, n), F32), jax.ShapeDtypeStruct((m, n), BF16)],
        scratch_shapes=[pltpu.VMEM((tm, n), F32), pltpu.VMEM((tm, n), F32)],
        compiler_params=_params("arbitrary", "arbitrary"),
        name="res_ln",
    )(a, w, x, g.reshape(1, n), b.reshape(1, n))


def _band_kernel(*refs, n_back, kvh, has_sink, with_lse):
    refs = list(refs)
    sink_ref = refs.pop(0) if has_sink else None
    q_ref, kp_ref, kc_ref, vp_ref, vc_ref, o_ref = refs[:6]
    lse_ref = refs[6] if with_lse else None
    i = pl.program_id(1)
    q = q_ref[...]
    kk = jnp.concatenate([kp_ref[...], kc_ref[...]], axis=0).astype(BF16)
    vv = jnp.concatenate([vp_ref[...], vc_ref[...]], axis=0).astype(BF16)
    qi = lax.broadcasted_iota(jnp.int32, (BLOCK, 2 * BLOCK), 0)
    ki = lax.broadcasted_iota(jnp.int32, (BLOCK, 2 * BLOCK), 1)
    dist = BLOCK + qi - ki
    valid = (dist >= 0) & (dist <= n_back) & ((ki >= BLOCK) | (i > 0))
    lane = lax.broadcasted_iota(jnp.int32, (BLOCK, LANES), 1)
    low = lane < HEAD_DIM
    if with_lse:
        lse_tile = jnp.zeros((BLOCK, LANES), F32)
    if has_sink:
        assert n_back < BLOCK
        sink_col = ki == 0
        not_row0 = lax.broadcasted_iota(jnp.int32, (2 * BLOCK, HEAD_DIM), 0) > 0
    zero = jnp.zeros((2 * BLOCK, HEAD_DIM), BF16)
    one = jnp.ones((2 * BLOCK, HEAD_DIM), BF16)
    nt = (((1,), (1,)), ((), ()))
    npair = GQA // 2
    for h in range(kvh):
        k_h = kk[:, h * HEAD_DIM:(h + 1) * HEAD_DIM]
        v_h = vv[:, h * HEAD_DIM:(h + 1) * HEAD_DIM]
        if has_sink:
            v_h = jnp.where(not_row0, v_h, zero)
        k_sel = (jnp.concatenate([k_h, zero], axis=1), jnp.concatenate([zero, k_h], axis=1))
        v_sel = (jnp.concatenate([v_h, zero, one, zero], axis=1),
                 jnp.concatenate([zero, v_h, zero, one], axis=1))
        q_pairs = jnp.concatenate(
            [q[:, (h * GQA + 2 * j) * HEAD_DIM:(h * GQA + 2 * j + 2) * HEAD_DIM] for j in range(npair)], axis=0)
        res = None
        maxes = []
        for par in range(2):
            s = lax.dot_general(q_pairs, k_sel[par], nt, preferred_element_type=F32)
            es = []
            for j in range(npair):
                fill = NEG_INF
                if has_sink:
                    fill = jnp.where(sink_col, sink_ref[h * GQA + 2 * j + par], NEG_INF)
                s_g = jnp.where(valid, s[j * BLOCK:(j + 1) * BLOCK], fill)
                m = jnp.max(s_g, axis=-1, keepdims=True)
                es.append(jnp.exp(s_g - m).astype(BF16))
                maxes.append(m)
            part = jnp.dot(jnp.concatenate(es, axis=0), v_sel[par], preferred_element_type=F32)
            res = part if res is None else res + part
        for j in range(npair):
            blk = res[j * BLOCK:(j + 1) * BLOCK]
            den = blk[:, LANES:]
            col = (h * GQA + 2 * j) * HEAD_DIM
            o_ref[:, col:col + LANES] = (blk[:, :LANES] * (1.0 / den)).astype(o_ref.dtype)
            if with_lse:
                m_pair = jnp.where(low, maxes[j], maxes[npair + j])
                pair_id = h * npair + j
                lse_tile = jnp.where(_mod(lane, HEAD_DIM) == pair_id, jnp.log(den) + m_pair, lse_tile)
    if with_lse:
        lse_ref[...] = lse_tile


def _band_attn(q, k, v, n_back, sinks=None, with_lse=False):
    n, length, c = q.shape
    ck = k.shape[2]
    kvh = ck // HEAD_DIM
    assert c == kvh * GQA * HEAD_DIM and length % BLOCK == 0 and n_back <= BLOCK
    nb = length // BLOCK
    cur = lambda b, i: (b, i, 0)
    prev = lambda b, i: (b, jnp.maximum(i - 1, 0), 0)
    in_specs = [
        pl.BlockSpec((None, BLOCK, c), cur),
        pl.BlockSpec((None, BLOCK, ck), prev), pl.BlockSpec((None, BLOCK, ck), cur),
        pl.BlockSpec((None, BLOCK, ck), prev), pl.BlockSpec((None, BLOCK, ck), cur),
    ]
    args = [q, k, k, v, v]
    if sinks is not None:
        in_specs = [pl.BlockSpec(memory_space=pltpu.SMEM)] + in_specs
        args = [sinks.astype(F32)] + args
    out_specs = [pl.BlockSpec((None, BLOCK, c), cur)]
    out_shape = [jax.ShapeDtypeStruct(q.shape, BF16)]
    if with_lse:
        out_specs.append(pl.BlockSpec((None, BLOCK, LANES), cur))
        out_shape.append(jax.ShapeDtypeStruct((n, length, LANES), F32))
    res = pl.pallas_call(
        functools.partial(_band_kernel, n_back=n_back, kvh=kvh, has_sink=sinks is not None, with_lse=with_lse),
        grid=(n, nb),
        in_specs=in_specs,
        out_specs=out_specs,
        out_shape=out_shape,
        compiler_params=_params("parallel", "arbitrary"),
        name="band_attn",
    )(*args)
    return res if with_lse else res[0]


def _permute_kernel(x_ref, o_ref):
    d = o_ref.shape[0]
    rows = o_ref.shape[1]
    for r in range(d):
        o_ref[r] = x_ref[pl.ds(r, rows, stride=d), :].astype(o_ref.dtype)


def _by_residue(x, d, out_dtype):
    n, t, c = x.shape
    return pl.pallas_call(
        _permute_kernel,
        grid=(n, c // LANES),
        in_specs=[pl.BlockSpec((None, t, LANES), lambda b, j: (b, 0, j))],
        out_specs=pl.BlockSpec((None, d, t // d, LANES), lambda b, j: (b, 0, 0, j)),
        out_shape=jax.ShapeDtypeStruct((n, d, t // d, c), out_dtype),
        compiler_params=_params("parallel", "parallel"),
        name="by_residue",
    )(x)


def _combine_kernel(*refs, dils):
    ng = len(dils)
    o_refs, l_refs, out_ref = refs[:ng], refs[ng:2 * ng], refs[2 * ng]
    stage = list(refs[2 * ng + 1:])
    c = out_ref.shape[1]
    nchunk = c // LANES
    outs, lses = [], []
    for o_ref, l_ref, d in zip(o_refs, l_refs, dils):
        if d == 1:
            outs.append(lambda j, o_ref=o_ref: o_ref[0, :, j * LANES:(j + 1) * LANES].astype(F32))
            lses.append(l_ref[0])
            continue
        o_st, l_st = stage.pop(0), stage.pop(0)
        rows = o_ref.shape[1]
        for r in range(d):
            l_st[pl.ds(r, rows, stride=d), :] = l_ref[r]
            for j in range(nchunk):
                o_st[j, pl.ds(r, rows, stride=d), :] = o_ref[r, :, j * LANES:(j + 1) * LANES].astype(F32)
        outs.append(lambda j, o_st=o_st: o_st[j])
        lses.append(l_st[...])
    top = functools.reduce(jnp.maximum, lses)
    ws = [jnp.exp(l - top) for l in lses]
    inv = 1.0 / functools.reduce(jnp.add, ws)
    his, los = [], []
    for w in ws:
        w = w * inv
        his.append(w.astype(BF16))
        los.append((w - his[-1].astype(F32)).astype(BF16))
    for j in range(nchunk):
        head_of_col = _div(lax.broadcasted_iota(jnp.int32, (LANES, LANES), 1) + j * LANES, HEAD_DIM)
        lane_of_col = _mod(head_of_col, 2) * HEAD_DIM + _div(head_of_col, 2)
        expand = (lane_of_col == lax.broadcasted_iota(jnp.int32, (LANES, LANES), 0)).astype(BF16)
        acc = None
        for o, hi, lo in zip(outs, his, los):
            wide = (jnp.dot(hi, expand, preferred_element_type=F32)
                    + jnp.dot(lo, expand, preferred_element_type=F32))
            term = wide * o(j)
            acc = term if acc is None else acc + term
        out_ref[:, j * LANES:(j + 1) * LANES] = acc.astype(out_ref.dtype)


def _combine(outs, lses, dils, tm=256):
    n, _, _, c = outs[0].shape
    t = outs[0].shape[1] * outs[0].shape[2]
    tm = _pick(t, tm)
    assert all(tm % (16 * d) == 0 for d in dils)
    blk = lambda width: [pl.BlockSpec((None, d, tm // d, width), lambda b, i: (b, 0, i, 0)) for d in dils]
    scratch = []
    for d in dils:
        if d > 1:
            scratch += [pltpu.VMEM((c // LANES, tm, LANES), F32), pltpu.VMEM((tm, LANES), F32)]
    return pl.pallas_call(
        functools.partial(_combine_kernel, dils=tuple(dils)),
        grid=(n, t // tm),
        in_specs=blk(c) + blk(LANES),
        out_specs=pl.BlockSpec((None, tm, c), lambda b, i: (b, i, 0)),
        out_shape=jax.ShapeDtypeStruct((n, t, c), BF16),
        scratch_shapes=scratch,
        compiler_params=_params("parallel", "parallel"),
        name="combine",
    )(*outs, *lses)


def _stack_heads(q, base, t):
    return jnp.concatenate([q[:, (base + g) * HEAD_DIM:(base + g + 1) * HEAD_DIM] for g in range(GQA)], axis=0)


def _sample_a_kernel(sink_ref, q_ref, kc_ref, kn_ref, vc_ref, vn_ref, o_ref, *, kvh, window):
    t = q_ref.shape[0]
    buf = kc_ref.shape[0]
    q = q_ref[...]
    kc = jnp.concatenate([kc_ref[...], kn_ref[...]], axis=0).astype(BF16)
    vc = jnp.concatenate([vc_ref[...], vn_ref[...]], axis=0).astype(BF16)
    rows = GQA * t
    row = lax.broadcasted_iota(jnp.int32, (rows, buf + t), 0)
    col = lax.broadcasted_iota(jnp.int32, (rows, buf + t), 1)
    dist = buf + _mod(row, t) - col
    valid = (dist >= 0) & (dist < window)
    rowg = _div(lax.broadcasted_iota(jnp.int32, (rows, 1), 0), t)
    for h in range(kvh):
        k_h = kc[:, h * HEAD_DIM:(h + 1) * HEAD_DIM]
        v_h = vc[:, h * HEAD_DIM:(h + 1) * HEAD_DIM]
        q_h = _stack_heads(q, h * GQA, t).astype(BF16)
        s = lax.dot_general(q_h, k_h, (((1,), (1,)), ((), ())), preferred_element_type=F32)
        s = jnp.where(valid, s, NEG_INF)
        sink = jnp.zeros((rows, 1), F32)
        for g in range(GQA):
            sink = jnp.where(rowg == g, sink_ref[h * GQA + g], sink)
        m = jnp.maximum(jnp.max(s, axis=-1, keepdims=True), sink)
        e = jnp.exp(s - m)
        den = jnp.sum(e, axis=-1, keepdims=True) + jnp.exp(sink - m)
        o = jnp.dot(e.astype(BF16), v_h, preferred_element_type=F32) * (1.0 / den)
        for g in range(GQA):
            col0 = (h * GQA + g) * HEAD_DIM
            o_ref[:, col0:col0 + HEAD_DIM] = o[g * t:(g + 1) * t].astype(o_ref.dtype)


def _sample_a_attn(q, kcache, knew, vcache, vnew, sinks):
    n, t, c = q.shape
    buf, ck = kcache.shape[1:]
    row3 = lambda b: (b, 0, 0)
    return pl.pallas_call(
        functools.partial(_sample_a_kernel, kvh=ck // HEAD_DIM, window=A_WINDOW),
        grid=(n,),
        in_specs=[pl.BlockSpec(memory_space=pltpu.SMEM),
                  pl.BlockSpec((None, t, c), row3),
                  pl.BlockSpec((None, buf, ck), row3), pl.BlockSpec((None, t, ck), row3),
                  pl.BlockSpec((None, buf, ck), row3), pl.BlockSpec((None, t, ck), row3)],
        out_specs=pl.BlockSpec((None, t, c), row3),
        out_shape=jax.ShapeDtypeStruct((n, t, c), F32),
        compiler_params=_params("parallel"),
        name="sample_a_attn",
    )(sinks.astype(F32), q, kcache, knew, vcache, vnew)


def _sample_b_kernel(q_ref, kc_ref, kn_ref, vc_ref, vn_ref, o_ref, *, kvh):
    t = q_ref.shape[0]
    buf = kc_ref.shape[0]
    c = o_ref.shape[1]
    rows = GQA * t
    q = q_ref[...]
    kn = kn_ref[...].astype(BF16)
    vn = vn_ref[...].astype(BF16)
    tok_n = _mod(lax.broadcasted_iota(jnp.int32, (rows, t), 0), t)
    dist_n = tok_n - lax.broadcasted_iota(jnp.int32, (rows, t), 1)
    for h in range(kvh):
        hs = slice(h * HEAD_DIM, (h + 1) * HEAD_DIM)
        kn_h, vn_h = kn[:, hs], vn[:, hs]
        outs, lses = [], []
        for gi, (window, d) in enumerate(B_GROUPS):
            start = max(buf - window, 0)
            span = buf - start
            kc_h = kc_ref[start:buf, hs].astype(BF16)
            vc_h = vc_ref[start:buf, hs].astype(BF16)
            q_h = _stack_heads(q, gi * (c // HEAD_DIM) + h * GQA, t).astype(BF16)
            nt = (((1,), (1,)), ((), ()))
            s_c = lax.dot_general(q_h, kc_h, nt, preferred_element_type=F32)
            s_n = lax.dot_general(q_h, kn_h, nt, preferred_element_type=F32)
            tok = _mod(lax.broadcasted_iota(jnp.int32, (rows, span), 0), t)
            dist_c = buf + tok - (start + lax.broadcasted_iota(jnp.int32, (rows, span), 1))
            s_c = jnp.where((_mod(dist_c, d) == 0) & (dist_c <= window), s_c, NEG_INF)
            s_n = jnp.where((dist_n >= 0) & (_mod(dist_n, d) == 0) & (dist_n <= window), s_n, NEG_INF)
            m = jnp.maximum(jnp.max(s_c, axis=-1, keepdims=True), jnp.max(s_n, axis=-1, keepdims=True))
            e_c = jnp.exp(s_c - m)
            e_n = jnp.exp(s_n - m)
            den = jnp.sum(e_c, axis=-1, keepdims=True) + jnp.sum(e_n, axis=-1, keepdims=True)
            pv = (jnp.dot(e_c.astype(BF16), vc_h, preferred_element_type=F32)
                  + jnp.dot(e_n.astype(BF16), vn_h, preferred_element_type=F32))
            outs.append(pv * (1.0 / den))
            lses.append(jnp.log(den) + m)
        top = functools.reduce(jnp.maximum, lses)
        ws = [jnp.exp(l - top) for l in lses]
        inv = 1.0 / functools.reduce(jnp.add, ws)
        o = functools.reduce(jnp.add, [o_g * (w * inv) for o_g, w in zip(outs, ws)])
        for g in range(GQA):
            col0 = (h * GQA + g) * HEAD_DIM
            o_ref[:, col0:col0 + HEAD_DIM] = o[g * t:(g + 1) * t].astype(o_ref.dtype)


def _sample_b_attn(q, kcache, knew, vcache, vnew):
    n, t, gc = q.shape
    c = gc // len(B_GROUPS)
    buf, ck = kcache.shape[1:]
    row3 = lambda b: (b, 0, 0)
    return pl.pallas_call(
        functools.partial(_sample_b_kernel, kvh=ck // HEAD_DIM),
        grid=(n,),
        in_specs=[pl.BlockSpec((None, t, gc), row3),
                  pl.BlockSpec((None, buf, ck), row3), pl.BlockSpec((None, t, ck), row3),
                  pl.BlockSpec((None, buf, ck), row3), pl.BlockSpec((None, t, ck), row3)],
        out_specs=pl.BlockSpec((None, t, c), row3),
        out_shape=jax.ShapeDtypeStruct((n, t, c), F32),
        compiler_params=_params("parallel"),
        name="sample_b_attn",
    )(q, kcache, knew, vcache, vnew)


def _rope_tables(pos):
    half = HEAD_DIM // 2
    inv = ROPE_THETA ** (-jnp.arange(half, dtype=F32) / half)
    ang = pos.astype(F32)[:, None] * inv[None, :]
    cos, sin = jnp.cos(ang), jnp.sin(ang)
    reps = LANES // HEAD_DIM
    return jnp.tile(jnp.concatenate([cos, cos], axis=1), (1, reps)), jnp.tile(jnp.concatenate([-sin, sin], axis=1), (1, reps))


def _trunk(x, pos, sample_caches, wts, w16):
    (ln_g, ln_b, w_qkv_a, sinks_a, w_o_a, w_kv_b, w_q_b, w_o_b, w_up, w_down) = wts
    n, t, dm = x.shape
    m = n * t
    depth = ln_g.shape[0]
    n_a = w_qkv_a.shape[0]
    alpha = (2 * depth) ** 0.25
    qa = w_o_a.shape[1]
    kva = (w_qkv_a.shape[2] - qa) // 2
    kvb = w_kv_b.shape[1] // 2
    cb = w_o_b.shape[1]
    ng = len(B_GROUPS)
    scale = HEAD_DIM ** -0.5
    is_prompt = sample_caches is None
    rope = _rope_tables(pos)
    if is_prompt:
        rope_by_d = {d: rope if d == 1 else _rope_tables(pos.reshape(t // d, d).T.reshape(t)) for _, d in B_GROUPS}
    if not is_prompt:
        rope = tuple(jnp.tile(r, (n, 1)) for r in rope)
    qdt = BF16 if is_prompt else F32

    def project(key, xin, w, col0, ncols, out_dtype, **kw):
        if is_prompt:
            out, w16[key] = _proj(xin, w, out_dtype, col0=col0, n=ncols, **kw)
            return out
        return _proj(xin, w16[key], out_dtype, **kw)

    x32 = x.reshape(m, dm)
    x16 = x32.astype(BF16)
    a_states = []
    b_state = None
    b_kv = None
    for layer in range(depth):
        if layer < n_a:
            q = project(("q_a", layer), x16, w_qkv_a[layer], 0, qa, qdt, rope=rope, rope_cols=qa, scale=scale)
            kv = project(("kv_a", layer), x16, w_qkv_a[layer], qa, 2 * kva, F32, rope=rope, rope_cols=kva)
            k, v = kv[:, :kva], kv[:, kva:]
            if is_prompt:
                k3, v3 = k.reshape(n, t, kva), v.reshape(n, t, kva)
                o = _band_attn(q.reshape(n, t, qa), k3, v3, A_WINDOW - 1, sinks=sinks_a[layer]).reshape(m, qa)
                keep = min(A_WINDOW, t)
                a_states.append((k3[:, -keep:], v3[:, -keep:]))
            else:
                ck = sample_caches[0][layer].reshape(n, -1, kva)
                cv = sample_caches[1][layer].reshape(n, -1, kva)
                k3, v3 = k.reshape(n, t, kva), v.reshape(n, t, kva)
                o = _sample_a_attn(q.reshape(n, t, qa), ck, k3, cv, v3, sinks_a[layer]).reshape(m, qa)
                keep = ck.shape[1]
                a_states.append((jnp.concatenate([ck, k3], axis=1)[:, -keep:],
                                 jnp.concatenate([cv, v3], axis=1)[:, -keep:]))
            x32, x16 = _res_ln(o, w_o_a[layer], x32, ln_g[layer, 0], ln_b[layer, 0], alpha)
        else:
            j = layer - n_a
            if j == 0:
                kv = project("kv_b", x16, w_kv_b, 0, 2 * kvb, F32, rope=rope, rope_cols=kvb)
                k3, v3 = kv[:, :kvb].reshape(n, t, kvb), kv[:, kvb:].reshape(n, t, kvb)
                if is_prompt:
                    keep = min(max(w for w, _ in B_GROUPS), t)
                    b_state = (k3[:, -keep:], v3[:, -keep:])
                    b_kv = [(k3, v3) if d == 1 else
                            tuple(_by_residue(a, d, F32).reshape(n * d, t // d, kvb) for a in (k3, v3))
                            for _, d in B_GROUPS]
                else:
                    ck = sample_caches[2].reshape(n, -1, kvb)
                    cv = sample_caches[3].reshape(n, -1, kvb)
                    keep = ck.shape[1]
                    b_state = (jnp.concatenate([ck, k3], axis=1)[:, -keep:],
                               jnp.concatenate([cv, v3], axis=1)[:, -keep:])
                    b_kv = (ck, k3, cv, v3)
            if is_prompt:
                outs, lses = [], []
                for gi, (window, d) in enumerate(B_GROUPS):
                    xg = x16 if d == 1 else _by_residue(x32.reshape(n, t, dm), d, BF16).reshape(m, dm)
                    q = project(("q_b", j, gi), xg, w_q_b[j], gi * cb, cb, BF16,
                                rope=rope_by_d[d], rope_cols=cb, scale=scale)
                    o_g, lse_g = _band_attn(q.reshape(n * d, t // d, cb), b_kv[gi][0], b_kv[gi][1],
                                            window // d, with_lse=True)
                    outs.append(o_g.reshape(n, d, t // d, cb))
                    lses.append(lse_g.reshape(n, d, t // d, LANES))
                o = _combine(outs, lses, [d for _, d in B_GROUPS]).reshape(m, cb)
            else:
                q = jnp.concatenate([project(("q_b", j, gi), x16, None, 0, cb, qdt, rope=rope, rope_cols=cb, scale=scale)
                                     for gi in range(ng)], axis=1)
                o = _sample_b_attn(q.reshape(n, t, ng * cb), *b_kv).reshape(m, cb)
            x32, x16 = _res_ln(o, w_o_b[j], x32, ln_g[layer, 0], ln_b[layer, 0], alpha)
        hid = project(("up", layer), x16, w_up[layer], 0, w_up.shape[2], BF16, relu2=True)
        x32, x16 = _res_ln(hid, w_down[layer], x32, ln_g[layer, 1], ln_b[layer, 1], alpha)
    return x32.reshape(n, t, dm), a_states, b_state


def kernel(x_prompt, x_sample, cache_a_k, cache_a_v, cache_b_k, cache_b_v, ln_g, ln_b,
           w_qkv_a, sinks_a, w_o_a, w_kv_b, w_q_b, w_o_b, w_up, w_down):
    wts = (ln_g, ln_b, w_qkv_a, sinks_a, w_o_a.astype(BF16), w_kv_b, w_q_b, w_o_b.astype(BF16), w_up,
           w_down.astype(BF16))
    w16 = {}
    kv_heads_a = cache_a_k.shape[3]
    kv_heads_b = cache_b_k.shape[2]
    pos_prompt = jnp.arange(x_prompt.shape[1], dtype=jnp.int32)
    pos_sample = PAST_LEN + jnp.arange(x_sample.shape[1], dtype=jnp.int32)
    y_prompt, a_prompt, b_prompt = _trunk(x_prompt, pos_prompt, None, wts, w16)
    y_sample, a_sample, b_sample = _trunk(x_sample, pos_sample, (cache_a_k, cache_a_v, cache_b_k, cache_b_v), wts, w16)

    def heads(s, kvh):
        return s.reshape(*s.shape[:-1], kvh, HEAD_DIM)

    return (y_prompt, y_sample,
            jnp.stack([heads(s[0], kv_heads_a) for s in a_prompt], axis=0),
            jnp.stack([heads(s[1], kv_heads_a) for s in a_prompt], axis=0),
            jnp.stack([heads(s[0], kv_heads_a) for s in a_sample], axis=0),
            jnp.stack([heads(s[1], kv_heads_a) for s in a_sample], axis=0),
            heads(b_prompt[0], kv_heads_b), heads(b_prompt[1], kv_heads_b),
            heads(b_sample[0], kv_heads_b), heads(b_sample[1], kv_heads_b))
```

```python
import functools

import jax
import jax.numpy as jnp
from jax import lax
from jax.experimental import pallas as pl
from jax.experimental.pallas import tpu as pltpu

HEAD_DIM = 64
GQA = 8
A_WINDOW = 128
B_GROUPS = ((128, 1), (512, 4), (2048, 16))
PAST_LEN = 16384
ROPE_THETA = 10000.0
LN_EPS = 1e-5
BLOCK = 128
NEG_INF = -1e30
LANES = 128
VMEM_LIMIT = 56 * 1024 * 1024

LN_ROWS = 16

BF16 = jnp.bfloat16
F32 = jnp.float32


def _mod(x, n):
    assert n & (n - 1) == 0, "power-of-two modulus only"
    return x & (n - 1)


def _div(x, n):
    assert n & (n - 1) == 0, "power-of-two divisor only"
    return x >> (n.bit_length() - 1)


def _params(*sem):
    return pltpu.CompilerParams(dimension_semantics=sem, vmem_limit_bytes=VMEM_LIMIT)


def _pick(n, pref):
    if n <= pref:
        return n
    t = pref
    while n % t:
        t //= 2
    return t


def _rope_chunk(x, cos, sin_signed):
    lane = lax.broadcasted_iota(jnp.int32, x.shape, 1)
    first = _mod(lane, HEAD_DIM) < (HEAD_DIM // 2)
    partner = jnp.where(first, pltpu.roll(x, LANES - HEAD_DIM // 2, 1), pltpu.roll(x, HEAD_DIM // 2, 1))
    return x * cos + partner * sin_signed


def _proj_kernel(*refs, rope_chunks, scale, relu2, cast_w, has_side):
    refs = list(refs)
    a_ref, w_ref = refs[:2]
    del refs[:2]
    cos_ref, sin_ref = (refs.pop(0), refs.pop(0)) if rope_chunks else (None, None)
    if has_side:
        side_ref = refs.pop(0)
    o_ref = refs.pop(0)
    if has_side:
        side_out_ref = refs.pop(1 if cast_w else 0)
        side_out_ref[...] = side_ref[...].astype(BF16)
    if cast_w:
        w16_out_ref, w16_ref = refs

        @pl.when(pl.program_id(1) == 0)
        def _():
            w16 = w_ref[...].astype(BF16)
            w16_ref[...] = w16
            w16_out_ref[...] = w16

        w = w16_ref[...]
    else:
        w = w_ref[...]
    acc = jnp.dot(a_ref[...].astype(BF16), w, preferred_element_type=F32)
    if relu2:
        acc = jnp.square(jnp.maximum(acc, 0.0))
    if not rope_chunks:
        o_ref[...] = acc.astype(o_ref.dtype)
        return
    cos = cos_ref[...]
    sin = sin_ref[...]
    for c in range(acc.shape[1] // LANES):
        x = acc[:, c * LANES:(c + 1) * LANES]
        if c < rope_chunks:
            x = _rope_chunk(x, cos, sin)
            if scale != 1.0:
                x = x * scale
        o_ref[:, c * LANES:(c + 1) * LANES] = x.astype(o_ref.dtype)


def _proj(a, w, out_dtype, *, layer=None, col0=0, n=None, rope=None, rope_cols=0, scale=1.0, relu2=False,
          side=None, tm=1024, tn=1024):
    m, k = a.shape
    n = w.shape[-1] if n is None else n
    tm = _pick(m, tm)
    tn = _pick(n, tn)
    assert col0 % tn == 0
    joff = col0 // tn
    cast_w = w.dtype != BF16
    rope_chunks = 0
    if rope is not None:
        cos, sin = rope
        p = cos.shape[0]
        tm = _pick(p, tm)
        pb = p // tm
        assert rope_cols == n or tn == n
        rope_chunks = min(rope_cols, tn) // LANES
    nj, ni = n // tn, m // tm
    if w.ndim == 3:
        w_spec = pl.BlockSpec((None, k, tn), lambda j, i: (layer, 0, j + joff))
    else:
        w_spec = pl.BlockSpec((k, tn), lambda j, i: (0, j + joff))
    in_specs = [pl.BlockSpec((tm, k), lambda j, i: (i, 0)), w_spec]
    args = [a, w]
    if rope is not None:
        in_specs += [pl.BlockSpec((tm, LANES), lambda j, i: (i % pb, 0))] * 2
        args += [cos, sin]
    out_specs = [pl.BlockSpec((tm, tn), lambda j, i: (i, j))]
    out_shape = [jax.ShapeDtypeStruct((m, n), out_dtype)]
    scratch = []
    if cast_w:
        out_specs.append(pl.BlockSpec((k, tn), lambda j, i: (0, j)))
        out_shape.append(jax.ShapeDtypeStruct((k, n), BF16))
        scratch.append(pltpu.VMEM((k, tn), BF16))
    if side is not None:
        w_side, side_layer = side
        _, rows, cols = w_side.shape
        rs = rows // (nj * ni)
        assert rs * nj * ni == rows and rs % 16 == 0
        in_specs.append(pl.BlockSpec((None, rs, cols), lambda j, i: (side_layer, j * ni + i, 0)))
        args.append(w_side)
        out_specs.append(pl.BlockSpec((rs, cols), lambda j, i: (j * ni + i, 0)))
        out_shape.append(jax.ShapeDtypeStruct((rows, cols), BF16))
    res = pl.pallas_call(
        functools.partial(_proj_kernel, rope_chunks=rope_chunks, scale=scale, relu2=relu2, cast_w=cast_w,
                          has_side=side is not None),
        grid=(nj, ni),
        in_specs=in_specs,
        out_specs=out_specs,
        out_shape=out_shape,
        scratch_shapes=scratch,
        compiler_params=_params("parallel", "arbitrary"),
        name="proj",
    )(*args)
    return tuple(res) if len(res) > 1 else res[0]


def _res_ln_kernel(a_ref, w_ref, x_ref, g_ref, b_ref, o32_ref, o16_ref, acc0_ref, acc1_ref, *, alpha, mt, nk):
    i = pl.program_id(0)
    kk = pl.program_id(1)
    accs = (acc0_ref, acc1_ref)
    rc = acc0_ref.shape[0] // nk

    def matmul(acc_ref):
        acc_ref[...] += jnp.dot(a_ref[...].astype(BF16), w_ref[...], preferred_element_type=F32)

    def layernorm(acc_ref):
        gain = g_ref[...]
        bias = b_ref[...]
        for s in range(rc // LN_ROWS):
            rows = pl.ds(pl.multiple_of(kk * rc + s * LN_ROWS, LN_ROWS), LN_ROWS)
            v = alpha * x_ref[rows, :] + acc_ref[rows, :]
            acc_ref[rows, :] = jnp.zeros((LN_ROWS, acc_ref.shape[1]), F32)
            mu = jnp.mean(v, axis=-1, keepdims=True)
            c = v - mu
            var = jnp.mean(c * c, axis=-1, keepdims=True)
            y = c * lax.rsqrt(var + LN_EPS) * gain + bias
            o32_ref[rows, :] = y
            o16_ref[rows, :] = y.astype(BF16)

    @pl.when((i == 0) & (kk == 0))
    def _():
        acc0_ref[...] = jnp.zeros_like(acc0_ref)
        acc1_ref[...] = jnp.zeros_like(acc1_ref)

    @pl.when(i == 0)
    def _():
        matmul(acc0_ref)

    for parity in range(2):
        @pl.when((i > 0) & (i < mt) & (i % 2 == parity))
        def _():
            matmul(accs[parity])
            layernorm(accs[1 - parity])

    @pl.when(i == mt)
    def _():
        layernorm(accs[(mt - 1) % 2])


def _res_ln(a, w, x, g, b, alpha, *, tm=512, tk=2048):
    m, k = a.shape
    n = w.shape[1]
    tm = _pick(m, tm)
    tk = _pick(k, tk)
    while tk < k and (tm // (k // tk)) % LN_ROWS:
        tk *= 2
    mt, nk = m // tm, k // tk
    assert (tm // nk) % LN_ROWS == 0 and tm % nk == 0
    a_map = lambda i, kk: (jnp.minimum(i, mt - 1), jnp.where(i < mt, kk, nk - 1))
    w_map = lambda i, kk: (jnp.where(i < mt, kk, nk - 1), 0)
    prev_map = lambda i, kk: (jnp.maximum(i - 1, 0), 0)
    return pl.pallas_call(
        functools.partial(_res_ln_kernel, alpha=alpha, mt=mt, nk=nk),
        grid=(mt + 1, nk),
        in_specs=[
            pl.BlockSpec((tm, tk), a_map),
            pl.BlockSpec((tk, n), w_map),
            pl.BlockSpec((tm, n), prev_map),
            pl.BlockSpec((1, n), lambda i, kk: (0, 0)),
            pl.BlockSpec((1, n), lambda i, kk: (0, 0)),
        ],
        out_specs=[pl.BlockSpec((tm, n), prev_map), pl.BlockSpec((tm, n), prev_map)],
        out_shape=[jax.ShapeDtypeStruct((m, n), F32), jax.ShapeDtypeStruct((m, n), BF16)],
        scratch_shapes=[pltpu.VMEM((tm, n), F32), pltpu.VMEM((tm, n), F32)],
        compiler_params=_params("arbitrary", "arbitrary"),
        name="res_ln",
    )(a, w, x, g.reshape(1, n), b.reshape(1, n))


def _band_kernel(*refs, n_back, kvh, has_sink, with_lse):
    refs = list(refs)
    sink_ref = refs.pop(0) if has_sink else None
    q_ref, kp_ref, kc_ref, vp_ref, vc_ref, o_ref = refs[:6]
    lse_ref = refs[6] if with_lse else None
    i = pl.program_id(1)
    q = q_ref[...]
    kk = jnp.concatenate([kp_ref[...], kc_ref[...]], axis=0).astype(BF16)
    vv = jnp.concatenate([vp_ref[...], vc_ref[...]], axis=0).astype(BF16)
    qi = lax.broadcasted_iota(jnp.int32, (BLOCK, 2 * BLOCK), 0)
    ki = lax.broadcasted_iota(jnp.int32, (BLOCK, 2 * BLOCK), 1)
    dist = BLOCK + qi - ki
    valid = (dist >= 0) & (dist <= n_back) & ((ki >= BLOCK) | (i > 0))
    lane = lax.broadcasted_iota(jnp.int32, (BLOCK, LANES), 1)
    low = lane < HEAD_DIM
    if with_lse:
        lse_tile = jnp.zeros((BLOCK, LANES), F32)
    if has_sink:
        assert n_back < BLOCK
        sink_col = ki == 0
        not_row0 = lax.broadcasted_iota(jnp.int32, (2 * BLOCK, HEAD_DIM), 0) > 0
    zero = jnp.zeros((2 * BLOCK, HEAD_DIM), BF16)
    one = jnp.ones((2 * BLOCK, HEAD_DIM), BF16)
    nt = (((1,), (1,)), ((), ()))
    npair = GQA // 2
    scores = []
    for h in range(kvh):
        k_h = kk[:, h * HEAD_DIM:(h + 1) * HEAD_DIM]
        k_sel = (jnp.concatenate([k_h, zero], axis=1), jnp.concatenate([zero, k_h], axis=1))
        q_pairs = jnp.concatenate(
            [q[:, (h * GQA + 2 * j) * HEAD_DIM:(h * GQA + 2 * j + 2) * HEAD_DIM] for j in range(npair)], axis=0)
        scores.append([lax.dot_general(q_pairs, k_sel[par], nt, preferred_element_type=F32) for par in range(2)])
    maxes, probs = [], []
    for h in range(kvh):
        maxes.append([])
        probs.append([])
        for par in range(2):
            es = []
            for j in range(npair):
                fill = NEG_INF
                if has_sink:
                    fill = jnp.where(sink_col, sink_ref[h * GQA + 2 * j + par], NEG_INF)
                s_g = jnp.where(valid, scores[h][par][j * BLOCK:(j + 1) * BLOCK], fill)
                m = jnp.max(s_g, axis=-1, keepdims=True)
                es.append(jnp.exp(s_g - m).astype(BF16))
                maxes[h].append(m)
            probs[h].append(jnp.concatenate(es, axis=0))
    results = []
    for h in range(kvh):
        v_h = vv[:, h * HEAD_DIM:(h + 1) * HEAD_DIM]
        if has_sink:
            v_h = jnp.where(not_row0, v_h, zero)
        v_sel = (jnp.concatenate([v_h, zero, one, zero], axis=1),
                 jnp.concatenate([zero, v_h, zero, one], axis=1))
        results.append(jnp.dot(probs[h][0], v_sel[0], preferred_element_type=F32)
                       + jnp.dot(probs[h][1], v_sel[1], preferred_element_type=F32))
    for h in range(kvh):
        for j in range(npair):
            blk = results[h][j * BLOCK:(j + 1) * BLOCK]
            den = blk[:, LANES:]
            col = (h * GQA + 2 * j) * HEAD_DIM
            o_ref[:, col:col + LANES] = (blk[:, :LANES] * (1.0 / den)).astype(o_ref.dtype)
            if with_lse:
                m_pair = jnp.where(low, maxes[h][j], maxes[h][npair + j])
                pair_id = h * npair + j
                lse_tile = jnp.where(_mod(lane, HEAD_DIM) == pair_id, jnp.log(den) + m_pair, lse_tile)
    if with_lse:
        lse_ref[...] = lse_tile


def _band_attn(q, k, v, n_back, sinks=None, with_lse=False):
    n, length, c = q.shape
    ck = k.shape[2]
    kvh = ck // HEAD_DIM
    assert c == kvh * GQA * HEAD_DIM and length % BLOCK == 0 and n_back <= BLOCK
    nb = length // BLOCK
    cur = lambda b, i: (b, i, 0)
    prev = lambda b, i: (b, jnp.maximum(i - 1, 0), 0)
    in_specs = [
        pl.BlockSpec((None, BLOCK, c), cur),
        pl.BlockSpec((None, BLOCK, ck), prev), pl.BlockSpec((None, BLOCK, ck), cur),
        pl.BlockSpec((None, BLOCK, ck), prev), pl.BlockSpec((None, BLOCK, ck), cur),
    ]
    args = [q, k, k, v, v]
    if sinks is not None:
        in_specs = [pl.BlockSpec(memory_space=pltpu.SMEM)] + in_specs
        args = [sinks.astype(F32)] + args
    out_specs = [pl.BlockSpec((None, BLOCK, c), cur)]
    out_shape = [jax.ShapeDtypeStruct(q.shape, BF16)]
    if with_lse:
        out_specs.append(pl.BlockSpec((None, BLOCK, LANES), cur))
        out_shape.append(jax.ShapeDtypeStruct((n, length, LANES), F32))
    res = pl.pallas_call(
        functools.partial(_band_kernel, n_back=n_back, kvh=kvh, has_sink=sinks is not None, with_lse=with_lse),
        grid=(n, nb),
        in_specs=in_specs,
        out_specs=out_specs,
        out_shape=out_shape,
        compiler_params=_params("parallel", "arbitrary"),
        name="band_attn",
    )(*args)
    return res if with_lse else res[0]


def _permute_kernel(x_ref, o_ref):
    d = o_ref.shape[0]
    rows = o_ref.shape[1]
    for r in range(d):
        o_ref[r] = x_ref[pl.ds(r, rows, stride=d), :].astype(o_ref.dtype)


def _by_residue(x, d, out_dtype):
    n, t, c = x.shape
    return pl.pallas_call(
        _permute_kernel,
        grid=(n, c // LANES),
        in_specs=[pl.BlockSpec((None, t, LANES), lambda b, j: (b, 0, j))],
        out_specs=pl.BlockSpec((None, d, t // d, LANES), lambda b, j: (b, 0, 0, j)),
        out_shape=jax.ShapeDtypeStruct((n, d, t // d, c), out_dtype),
        compiler_params=_params("parallel", "parallel"),
        name="by_residue",
    )(x)


def _combine_kernel(*refs, dils):
    ng = len(dils)
    o_refs, l_refs, out_ref = refs[:ng], refs[ng:2 * ng], refs[2 * ng]
    stage = list(refs[2 * ng + 1:])
    c = out_ref.shape[1]
    nchunk = c // LANES
    outs, lses = [], []
    for o_ref, l_ref, d in zip(o_refs, l_refs, dils):
        if d == 1:
            outs.append(lambda j, o_ref=o_ref: o_ref[0, :, j * LANES:(j + 1) * LANES].astype(F32))
            lses.append(l_ref[0])
            continue
        o_st, l_st = stage.pop(0), stage.pop(0)
        rows = o_ref.shape[1]
        for r in range(d):
            l_st[pl.ds(r, rows, stride=d), :] = l_ref[r]
            for j in range(nchunk):
                o_st[j, pl.ds(r, rows, stride=d), :] = o_ref[r, :, j * LANES:(j + 1) * LANES].astype(F32)
        outs.append(lambda j, o_st=o_st: o_st[j])
        lses.append(l_st[...])
    top = functools.reduce(jnp.maximum, lses)
    ws = [jnp.exp(l - top) for l in lses]
    inv = 1.0 / functools.reduce(jnp.add, ws)
    his, los = [], []
    for w in ws:
        w = w * inv
        his.append(w.astype(BF16))
        los.append((w - his[-1].astype(F32)).astype(BF16))
    for j in range(nchunk):
        head_of_col = _div(lax.broadcasted_iota(jnp.int32, (LANES, LANES), 1) + j * LANES, HEAD_DIM)
        lane_of_col = _mod(head_of_col, 2) * HEAD_DIM + _div(head_of_col, 2)
        expand = (lane_of_col == lax.broadcasted_iota(jnp.int32, (LANES, LANES), 0)).astype(BF16)
        acc = None
        for o, hi, lo in zip(outs, his, los):
            wide = (jnp.dot(hi, expand, preferred_element_type=F32)
                    + jnp.dot(lo, expand, preferred_element_type=F32))
            term = wide * o(j)
            acc = term if acc is None else acc + term
        out_ref[:, j * LANES:(j + 1) * LANES] = acc.astype(out_ref.dtype)


def _combine(outs, lses, dils, tm=256):
    n, _, _, c = outs[0].shape
    t = outs[0].shape[1] * outs[0].shape[2]
    tm = _pick(t, tm)
    assert all(tm % (16 * d) == 0 for d in dils)
    blk = lambda width: [pl.BlockSpec((None, d, tm // d, width), lambda b, i: (b, 0, i, 0)) for d in dils]
    scratch = []
    for d in dils:
        if d > 1:
            scratch += [pltpu.VMEM((c // LANES, tm, LANES), F32), pltpu.VMEM((tm, LANES), F32)]
    return pl.pallas_call(
        functools.partial(_combine_kernel, dils=tuple(dils)),
        grid=(n, t // tm),
        in_specs=blk(c) + blk(LANES),
        out_specs=pl.BlockSpec((None, tm, c), lambda b, i: (b, i, 0)),
        out_shape=jax.ShapeDtypeStruct((n, t, c), BF16),
        scratch_shapes=scratch,
        compiler_params=_params("parallel", "parallel"),
        name="combine",
    )(*outs, *lses)


def _stack_heads(q, base, t):
    return jnp.concatenate([q[:, (base + g) * HEAD_DIM:(base + g + 1) * HEAD_DIM] for g in range(GQA)], axis=0)


def _sample_a_kernel(sink_ref, q_ref, kc_ref, kn_ref, vc_ref, vn_ref, o_ref, *, kvh, window):
    t = q_ref.shape[0]
    buf = kc_ref.shape[0]
    q = q_ref[...]
    kc = jnp.concatenate([kc_ref[...], kn_ref[...]], axis=0).astype(BF16)
    vc = jnp.concatenate([vc_ref[...], vn_ref[...]], axis=0).astype(BF16)
    rows = GQA * t
    row = lax.broadcasted_iota(jnp.int32, (rows, buf + t), 0)
    col = lax.broadcasted_iota(jnp.int32, (rows, buf + t), 1)
    dist = buf + _mod(row, t) - col
    valid = (dist >= 0) & (dist < window)
    rowg = _div(lax.broadcasted_iota(jnp.int32, (rows, 1), 0), t)
    for h in range(kvh):
        k_h = kc[:, h * HEAD_DIM:(h + 1) * HEAD_DIM]
        v_h = vc[:, h * HEAD_DIM:(h + 1) * HEAD_DIM]
        q_h = _stack_heads(q, h * GQA, t).astype(BF16)
        s = lax.dot_general(q_h, k_h, (((1,), (1,)), ((), ())), preferred_element_type=F32)
        s = jnp.where(valid, s, NEG_INF)
        sink = jnp.zeros((rows, 1), F32)
        for g in range(GQA):
            sink = jnp.where(rowg == g, sink_ref[h * GQA + g], sink)
        m = jnp.maximum(jnp.max(s, axis=-1, keepdims=True), sink)
        e = jnp.exp(s - m)
        den = jnp.sum(e, axis=-1, keepdims=True) + jnp.exp(sink - m)
        o = jnp.dot(e.astype(BF16), v_h, preferred_element_type=F32) * (1.0 / den)
        for g in range(GQA):
            col0 = (h * GQA + g) * HEAD_DIM
            o_ref[:, col0:col0 + HEAD_DIM] = o[g * t:(g + 1) * t].astype(o_ref.dtype)


def _sample_a_attn(q, kcache, knew, vcache, vnew, sinks):
    n, t, c = q.shape
    buf, ck = kcache.shape[1:]
    row3 = lambda b: (b, 0, 0)
    return pl.pallas_call(
        functools.partial(_sample_a_kernel, kvh=ck // HEAD_DIM, window=A_WINDOW),
        grid=(n,),
        in_specs=[pl.BlockSpec(memory_space=pltpu.SMEM),
                  pl.BlockSpec((None, t, c), row3),
                  pl.BlockSpec((None, buf, ck), row3), pl.BlockSpec((None, t, ck), row3),
                  pl.BlockSpec((None, buf, ck), row3), pl.BlockSpec((None, t, ck), row3)],
        out_specs=pl.BlockSpec((None, t, c), row3),
        out_shape=jax.ShapeDtypeStruct((n, t, c), F32),
        compiler_params=_params("parallel"),
        name="sample_a_attn",
    )(sinks.astype(F32), q, kcache, knew, vcache, vnew)


def _sample_b_kernel(q_ref, kt_ref, kn_ref, vt_ref, vn_ref, o_ref, *, kvh):
    t = q_ref.shape[0]
    buf = kt_ref.shape[1]
    c = o_ref.shape[1]
    rows = GQA * t
    q = q_ref[...]
    kn = kn_ref[...].astype(BF16)
    vn = vn_ref[...].astype(BF16)
    tok_n = _mod(lax.broadcasted_iota(jnp.int32, (rows, t), 0), t)
    dist_n = tok_n - lax.broadcasted_iota(jnp.int32, (rows, t), 1)
    ones_c = jnp.ones((HEAD_DIM, buf), BF16)
    ones_n = jnp.ones((t, HEAD_DIM), BF16)
    nt = (((1,), (1,)), ((), ()))
    starts = [max(buf - window, 0) for window, _ in B_GROUPS]
    pairs = [(h, gi) for h in range(kvh) for gi in range(len(B_GROUPS))]
    hsl = lambda h: slice(h * HEAD_DIM, (h + 1) * HEAD_DIM)
    scores = []
    for h, gi in pairs:
        if gi == 0:
            kt_h = kt_ref[hsl(h), :].astype(BF16)
        q_h = _stack_heads(q, gi * (c // HEAD_DIM) + h * GQA, t).astype(BF16)
        scores.append((jnp.dot(q_h, kt_h[:, starts[gi]:], preferred_element_type=F32),
                       lax.dot_general(q_h, kn[:, hsl(h)], nt, preferred_element_type=F32)))
    probs = []
    for (h, gi), (s_c, s_n) in zip(pairs, scores):
        window, d = B_GROUPS[gi]
        span = buf - starts[gi]
        tok = _mod(lax.broadcasted_iota(jnp.int32, (rows, span), 0), t)
        dist_c = buf + tok - (starts[gi] + lax.broadcasted_iota(jnp.int32, (rows, span), 1))
        s_c = jnp.where((_mod(dist_c, d) == 0) & (dist_c <= window), s_c, NEG_INF)
        s_n = jnp.where((dist_n >= 0) & (_mod(dist_n, d) == 0) & (dist_n <= window), s_n, NEG_INF)
        m = jnp.maximum(jnp.max(s_c, axis=-1, keepdims=True), jnp.max(s_n, axis=-1, keepdims=True))
        probs.append((jnp.exp(s_c - m).astype(BF16), jnp.exp(s_n - m).astype(BF16), m))
    results = []
    for (h, gi), (e_c, e_n, m) in zip(pairs, probs):
        if gi == 0:
            vt_h = jnp.concatenate([vt_ref[hsl(h), :].astype(BF16), ones_c, ones_c], axis=0)
            vn_h = jnp.concatenate([vn[:, hsl(h)], ones_n, ones_n], axis=1)
        res = (lax.dot_general(e_c, vt_h[:, starts[gi]:], nt, preferred_element_type=F32)
               + jnp.dot(e_n, vn_h, preferred_element_type=F32))
        results.append((res, m))
    for h in range(kvh):
        outs, lses = [], []
        for res, m in results[h * len(B_GROUPS):(h + 1) * len(B_GROUPS)]:
            den = res[:, LANES:LANES + HEAD_DIM]
            outs.append(res[:, :HEAD_DIM] * (1.0 / den))
            lses.append(jnp.log(den) + m)
        top = functools.reduce(jnp.maximum, lses)
        ws = [jnp.exp(l - top) for l in lses]
        inv = 1.0 / functools.reduce(jnp.add, ws)
        o = functools.reduce(jnp.add, [o_g * (w * inv) for o_g, w in zip(outs, ws)])
        for g in range(GQA):
            col0 = (h * GQA + g) * HEAD_DIM
            o_ref[:, col0:col0 + HEAD_DIM] = o[g * t:(g + 1) * t].astype(o_ref.dtype)


def _sample_b_attn(q, kt_cache, knew, vt_cache, vnew):
    n, t, gc = q.shape
    c = gc // len(B_GROUPS)
    ck, buf = kt_cache.shape[1:]
    assert all(max(buf - w, 0) % LANES == 0 for w, _ in B_GROUPS)
    row3 = lambda b: (b, 0, 0)
    return pl.pallas_call(
        functools.partial(_sample_b_kernel, kvh=ck // HEAD_DIM),
        grid=(n,),
        in_specs=[pl.BlockSpec((None, t, gc), row3),
                  pl.BlockSpec((None, ck, buf), row3), pl.BlockSpec((None, t, ck), row3),
                  pl.BlockSpec((None, ck, buf), row3), pl.BlockSpec((None, t, ck), row3)],
        out_specs=pl.BlockSpec((None, t, c), row3),
        out_shape=jax.ShapeDtypeStruct((n, t, c), F32),
        compiler_params=_params("parallel"),
        name="sample_b_attn",
    )(q, kt_cache, knew, vt_cache, vnew)


def _rope_tables(pos):
    half = HEAD_DIM // 2
    inv = ROPE_THETA ** (-jnp.arange(half, dtype=F32) / half)
    ang = pos.astype(F32)[:, None] * inv[None, :]
    cos, sin = jnp.cos(ang), jnp.sin(ang)
    reps = LANES // HEAD_DIM
    return jnp.tile(jnp.concatenate([cos, cos], axis=1), (1, reps)), jnp.tile(jnp.concatenate([-sin, sin], axis=1), (1, reps))


def _trunk(x, pos, sample_caches, wts, w16):
    (ln_g, ln_b, w_qkv_a, sinks_a, w_o_a, w_kv_b, w_q_b, w_o_b, w_up, w_down) = wts
    n, t, dm = x.shape
    m = n * t
    depth = ln_g.shape[0]
    n_a = w_qkv_a.shape[0]
    alpha = (2 * depth) ** 0.25
    qa = w_o_a.shape[1]
    kva = (w_qkv_a.shape[2] - qa) // 2
    kvb = w_kv_b.shape[1] // 2
    cb = w_o_b.shape[1]
    ng = len(B_GROUPS)
    scale = HEAD_DIM ** -0.5
    is_prompt = sample_caches is None
    rope = _rope_tables(pos)
    if is_prompt:
        rope_by_d = {d: rope if d == 1 else _rope_tables(pos.reshape(t // d, d).T.reshape(t)) for _, d in B_GROUPS}
    if not is_prompt:
        rope = tuple(jnp.tile(r, (n, 1)) for r in rope)
    qdt = BF16 if is_prompt else F32

    def project(key, xin, w, layer, col0, ncols, out_dtype, side=None, side_key=None, **kw):
        if is_prompt:
            res = _proj(xin, w, out_dtype, layer=layer, col0=col0, n=ncols, side=side, **kw)
            w16[key] = res[1]
            if side is not None:
                w16[side_key] = res[2]
            return res[0]
        return _proj(xin, w16[key], out_dtype, **kw)

    x32 = x.reshape(m, dm)
    x16 = x32.astype(BF16)
    a_states = []
    b_state = None
    b_kv = None
    for layer in range(depth):
        if layer < n_a:
            q = project(("q_a", layer), x16, w_qkv_a, layer, 0, qa, qdt, rope=rope, rope_cols=qa, scale=scale,
                        side=(w_o_a, layer), side_key=("o_a", layer))
            kv = project(("kv_a", layer), x16, w_qkv_a, layer, qa, 2 * kva, F32, rope=rope, rope_cols=kva)
            k, v = kv[:, :kva], kv[:, kva:]
            if is_prompt:
                k3, v3 = k.reshape(n, t, kva), v.reshape(n, t, kva)
                o = _band_attn(q.reshape(n, t, qa), k3, v3, A_WINDOW - 1, sinks=sinks_a[layer]).reshape(m, qa)
                keep = min(A_WINDOW, t)
                a_states.append((k3[:, -keep:], v3[:, -keep:]))
            else:
                ck = sample_caches[0][layer].reshape(n, -1, kva)
                cv = sample_caches[1][layer].reshape(n, -1, kva)
                k3, v3 = k.reshape(n, t, kva), v.reshape(n, t, kva)
                o = _sample_a_attn(q.reshape(n, t, qa), ck, k3, cv, v3, sinks_a[layer]).reshape(m, qa)
                keep = ck.shape[1]
                a_states.append((jnp.concatenate([ck, k3], axis=1)[:, -keep:],
                                 jnp.concatenate([cv, v3], axis=1)[:, -keep:]))
            x32, x16 = _res_ln(o, w16[("o_a", layer)], x32, ln_g[layer, 0], ln_b[layer, 0], alpha)
        else:
            j = layer - n_a
            if j == 0:
                kv = project("kv_b", x16, w_kv_b, None, 0, 2 * kvb, F32, rope=rope, rope_cols=kvb)
                k3, v3 = kv[:, :kvb].reshape(n, t, kvb), kv[:, kvb:].reshape(n, t, kvb)
                if is_prompt:
                    keep = min(max(w for w, _ in B_GROUPS), t)
                    b_state = (k3[:, -keep:], v3[:, -keep:])
                    b_kv = [(k3, v3) if d == 1 else
                            tuple(_by_residue(a, d, F32).reshape(n * d, t // d, kvb) for a in (k3, v3))
                            for _, d in B_GROUPS]
                else:
                    ck = sample_caches[2].reshape(n, -1, kvb)
                    cv = sample_caches[3].reshape(n, -1, kvb)
                    keep = ck.shape[1]
                    b_state = (jnp.concatenate([ck, k3], axis=1)[:, -keep:],
                               jnp.concatenate([cv, v3], axis=1)[:, -keep:])
                    b_kv = (jnp.swapaxes(ck, 1, 2), k3, jnp.swapaxes(cv, 1, 2), v3)
            if is_prompt:
                outs, lses = [], []
                for gi, (window, d) in enumerate(B_GROUPS):
                    xg = x16 if d == 1 else _by_residue(x32.reshape(n, t, dm), d, BF16).reshape(m, dm)
                    side = dict(side=(w_o_b, j), side_key=("o_b", j)) if gi == 0 else {}
                    q = project(("q_b", j, gi), xg, w_q_b, j, gi * cb, cb, BF16,
                                rope=rope_by_d[d], rope_cols=cb, scale=scale, **side)
                    o_g, lse_g = _band_attn(q.reshape(n * d, t // d, cb), b_kv[gi][0], b_kv[gi][1],
                                            window // d, with_lse=True)
                    outs.append(o_g.reshape(n, d, t // d, cb))
                    lses.append(lse_g.reshape(n, d, t // d, LANES))
                o = _combine(outs, lses, [d for _, d in B_GROUPS]).reshape(m, cb)
            else:
                q = jnp.concatenate([project(("q_b", j, gi), x16, None, None, 0, cb, qdt,
                                             rope=rope, rope_cols=cb, scale=scale) for gi in range(ng)], axis=1)
                o = _sample_b_attn(q.reshape(n, t, ng * cb), *b_kv).reshape(m, cb)
            x32, x16 = _res_ln(o, w16[("o_b", j)], x32, ln_g[layer, 0], ln_b[layer, 0], alpha)
        hid = project(("up", layer), x16, w_up, layer, 0, w_up.shape[2], BF16, relu2=True,
                      side=(w_down, layer), side_key=("down", layer))
        x32, x16 = _res_ln(hid, w16[("down", layer)], x32, ln_g[layer, 1], ln_b[layer, 1], alpha)
    return x32.reshape(n, t, dm), a_states, b_state


def kernel(x_prompt, x_sample, cache_a_k, cache_a_v, cache_b_k, cache_b_v, ln_g, ln_b,
           w_qkv_a, sinks_a, w_o_a, w_kv_b, w_q_b, w_o_b, w_up, w_down):
    wts = (ln_g, ln_b, w_qkv_a, sinks_a, w_o_a, w_kv_b, w_q_b, w_o_b, w_up, w_down)
    w16 = {}
    kv_heads_a = cache_a_k.shape[3]
    kv_heads_b = cache_b_k.shape[2]
    pos_prompt = jnp.arange(x_prompt.shape[1], dtype=jnp.int32)
    pos_sample = PAST_LEN + jnp.arange(x_sample.shape[1], dtype=jnp.int32)
    y_prompt, a_prompt, b_prompt = _trunk(x_prompt, pos_prompt, None, wts, w16)
    y_sample, a_sample, b_sample = _trunk(x_sample, pos_sample, (cache_a_k, cache_a_v, cache_b_k, cache_b_v), wts, w16)

    def heads(s, kvh):
        return s.reshape(*s.shape[:-1], kvh, HEAD_DIM)

    return (y_prompt, y_sample,
            jnp.stack([heads(s[0], kv_heads_a) for s in a_prompt], axis=0),
            jnp.stack([heads(s[1], kv_heads_a) for s in a_prompt], axis=0),
            jnp.stack([heads(s[0], kv_heads_a) for s in a_sample], axis=0),
            jnp.stack([heads(s[1], kv_heads_a) for s in a_sample], axis=0),
            heads(b_prompt[0], kv_heads_b), heads(b_prompt[1], kv_heads_b),
            heads(b_sample[0], kv_heads_b), heads(b_sample[1], kv_heads_b))
```

```python
import functools

import jax
import jax.numpy as jnp
from jax import lax
from jax.experimental import pallas as pl
from jax.experimental.pallas import tpu as pltpu

HEAD_DIM = 64
GQA = 8
A_WINDOW = 128
B_GROUPS = ((128, 1), (512, 4), (2048, 16))
PAST_LEN = 16384
ROPE_THETA = 10000.0
LN_EPS = 1e-5
BLOCK = 128
NEG_INF = -1e30
LANES = 128
VMEM_LIMIT = 56 * 1024 * 1024

LN_ROWS = 16

BF16 = jnp.bfloat16
F32 = jnp.float32


def _mod(x, n):
    assert n & (n - 1) == 0, "power-of-two modulus only"
    return x & (n - 1)


def _div(x, n):
    assert n & (n - 1) == 0, "power-of-two divisor only"
    return x >> (n.bit_length() - 1)


def _params(*sem):
    return pltpu.CompilerParams(dimension_semantics=sem, vmem_limit_bytes=VMEM_LIMIT)


def _pick(n, pref):
    if n <= pref:
        return n
    t = pref
    while n % t:
        t //= 2
    return t


def _rope_chunk(x, cos, sin_signed):
    lane = lax.broadcasted_iota(jnp.int32, x.shape, 1)
    first = _mod(lane, HEAD_DIM) < (HEAD_DIM // 2)
    partner = jnp.where(first, pltpu.roll(x, LANES - HEAD_DIM // 2, 1), pltpu.roll(x, HEAD_DIM // 2, 1))
    return x * cos + partner * sin_signed


def _proj_kernel(*refs, rope_chunks, scale, relu2, cast_w, has_side):
    refs = list(refs)
    a_ref, w_ref = refs[:2]
    del refs[:2]
    cos_ref, sin_ref = (refs.pop(0), refs.pop(0)) if rope_chunks else (None, None)
    if has_side:
        side_ref = refs.pop(0)
    o_ref = refs.pop(0)
    if has_side:
        side_out_ref = refs.pop(1 if cast_w else 0)
        side_out_ref[...] = side_ref[...].astype(BF16)
    if cast_w:
        w16_out_ref, w16_ref = refs

        @pl.when(pl.program_id(1) == 0)
        def _():
            w16 = w_ref[...].astype(BF16)
            w16_ref[...] = w16
            w16_out_ref[...] = w16

        w = w16_ref[...]
    else:
        w = w_ref[...]
    acc = jnp.dot(a_ref[...].astype(BF16), w, preferred_element_type=F32)
    if relu2:
        acc = jnp.square(jnp.maximum(acc, 0.0))
    if not rope_chunks:
        o_ref[...] = acc.astype(o_ref.dtype)
        return
    cos = cos_ref[...]
    sin = sin_ref[...]
    for c in range(acc.shape[1] // LANES):
        x = acc[:, c * LANES:(c + 1) * LANES]
        if c < rope_chunks:
            x = _rope_chunk(x, cos, sin)
            if scale != 1.0:
                x = x * scale
        o_ref[:, c * LANES:(c + 1) * LANES] = x.astype(o_ref.dtype)


def _proj(a, w, out_dtype, *, layer=None, col0=0, n=None, rope=None, rope_cols=0, scale=1.0, relu2=False,
          side=None, tm=1024, tn=1024):
    m, k = a.shape
    n = w.shape[-1] if n is None else n
    if m <= tm // 2:
        tn *= 2
    tm = _pick(m, tm)
    tn = _pick(n, tn)
    assert col0 % tn == 0
    joff = col0 // tn
    cast_w = w.dtype != BF16
    rope_chunks = 0
    if rope is not None:
        cos, sin = rope
        p = cos.shape[0]
        tm = _pick(p, tm)
        pb = p // tm
        assert rope_cols == n or tn == n
        rope_chunks = min(rope_cols, tn) // LANES
    nj, ni = n // tn, m // tm
    if w.ndim == 3:
        w_spec = pl.BlockSpec((None, k, tn), lambda j, i: (layer, 0, j + joff))
    else:
        w_spec = pl.BlockSpec((k, tn), lambda j, i: (0, j + joff))
    in_specs = [pl.BlockSpec((tm, k), lambda j, i: (i, 0)), w_spec]
    args = [a, w]
    if rope is not None:
        in_specs += [pl.BlockSpec((tm, LANES), lambda j, i: (i % pb, 0))] * 2
        args += [cos, sin]
    out_specs = [pl.BlockSpec((tm, tn), lambda j, i: (i, j))]
    out_shape = [jax.ShapeDtypeStruct((m, n), out_dtype)]
    scratch = []
    if cast_w:
        out_specs.append(pl.BlockSpec((k, tn), lambda j, i: (0, j)))
        out_shape.append(jax.ShapeDtypeStruct((k, n), BF16))
        scratch.append(pltpu.VMEM((k, tn), BF16))
    if side is not None:
        w_side, side_layer = side
        _, rows, cols = w_side.shape
        rs = rows // (nj * ni)
        assert rs * nj * ni == rows and rs % 16 == 0
        in_specs.append(pl.BlockSpec((None, rs, cols), lambda j, i: (side_layer, j * ni + i, 0)))
        args.append(w_side)
        out_specs.append(pl.BlockSpec((rs, cols), lambda j, i: (j * ni + i, 0)))
        out_shape.append(jax.ShapeDtypeStruct((rows, cols), BF16))
    res = pl.pallas_call(
        functools.partial(_proj_kernel, rope_chunks=rope_chunks, scale=scale, relu2=relu2, cast_w=cast_w,
                          has_side=side is not None),
        grid=(nj, ni),
        in_specs=in_specs,
        out_specs=out_specs,
        out_shape=out_shape,
        scratch_shapes=scratch,
        compiler_params=_params("parallel", "arbitrary"),
        name="proj",
    )(*args)
    return tuple(res) if len(res) > 1 else res[0]


def _res_ln_kernel(a_ref, w_ref, x_ref, g_ref, b_ref, o32_ref, o16_ref, acc0_ref, acc1_ref, *, alpha, mt, nk):
    i = pl.program_id(0)
    kk = pl.program_id(1)
    accs = (acc0_ref, acc1_ref)
    rc = acc0_ref.shape[0] // nk

    def matmul(acc_ref):
        acc_ref[...] += jnp.dot(a_ref[...].astype(BF16), w_ref[...], preferred_element_type=F32)

    def layernorm(acc_ref):
        gain = g_ref[...]
        bias = b_ref[...]
        for s in range(rc // LN_ROWS):
            rows = pl.ds(pl.multiple_of(kk * rc + s * LN_ROWS, LN_ROWS), LN_ROWS)
            v = alpha * x_ref[rows, :] + acc_ref[rows, :]
            acc_ref[rows, :] = jnp.zeros((LN_ROWS, acc_ref.shape[1]), F32)
            mu = jnp.mean(v, axis=-1, keepdims=True)
            c = v - mu
            var = jnp.mean(c * c, axis=-1, keepdims=True)
            y = c * lax.rsqrt(var + LN_EPS) * gain + bias
            o32_ref[rows, :] = y
            o16_ref[rows, :] = y.astype(BF16)

    @pl.when((i == 0) & (kk == 0))
    def _():
        acc0_ref[...] = jnp.zeros_like(acc0_ref)
        acc1_ref[...] = jnp.zeros_like(acc1_ref)

    @pl.when(i == 0)
    def _():
        matmul(acc0_ref)

    for parity in range(2):
        @pl.when((i > 0) & (i < mt) & (i % 2 == parity))
        def _():
            matmul(accs[parity])
            layernorm(accs[1 - parity])

    @pl.when(i == mt)
    def _():
        layernorm(accs[(mt - 1) % 2])


def _res_ln(a, w, x, g, b, alpha, *, tm=512, tk=2048):
    m, k = a.shape
    n = w.shape[1]
    tm = _pick(m, tm)
    tk = _pick(k, tk)
    while tk < k and (tm // (k // tk)) % LN_ROWS:
        tk *= 2
    mt, nk = m // tm, k // tk
    assert (tm // nk) % LN_ROWS == 0 and tm % nk == 0
    a_map = lambda i, kk: (jnp.minimum(i, mt - 1), jnp.where(i < mt, kk, nk - 1))
    w_map = lambda i, kk: (jnp.where(i < mt, kk, nk - 1), 0)
    prev_map = lambda i, kk: (jnp.maximum(i - 1, 0), 0)
    return pl.pallas_call(
        functools.partial(_res_ln_kernel, alpha=alpha, mt=mt, nk=nk),
        grid=(mt + 1, nk),
        in_specs=[
            pl.BlockSpec((tm, tk), a_map),
            pl.BlockSpec((tk, n), w_map),
            pl.BlockSpec((tm, n), prev_map),
            pl.BlockSpec((1, n), lambda i, kk: (0, 0)),
            pl.BlockSpec((1, n), lambda i, kk: (0, 0)),
        ],
        out_specs=[pl.BlockSpec((tm, n), prev_map), pl.BlockSpec((tm, n), prev_map)],
        out_shape=[jax.ShapeDtypeStruct((m, n), F32), jax.ShapeDtypeStruct((m, n), BF16)],
        scratch_shapes=[pltpu.VMEM((tm, n), F32), pltpu.VMEM((tm, n), F32)],
        compiler_params=_params("arbitrary", "arbitrary"),
        name="res_ln",
    )(a, w, x, g.reshape(1, n), b.reshape(1, n))


def _band_kernel(*refs, n_back, kvh, has_sink, with_lse, qb):
    refs = list(refs)
    sink_ref = refs.pop(0) if has_sink else None
    q_ref, kp_ref, kc_ref, vp_ref, vc_ref, o_ref = refs[:6]
    lse_ref = refs[6] if with_lse else None
    i = pl.program_id(1)
    k_all = jnp.concatenate([kp_ref[...], kc_ref[...]], axis=0).astype(BF16)
    v_all = jnp.concatenate([vp_ref[...], vc_ref[...]], axis=0).astype(BF16)
    qi = lax.broadcasted_iota(jnp.int32, (BLOCK, 2 * BLOCK), 0)
    ki = lax.broadcasted_iota(jnp.int32, (BLOCK, 2 * BLOCK), 1)
    dist = BLOCK + qi - ki
    in_band = (dist >= 0) & (dist <= n_back)
    lane = lax.broadcasted_iota(jnp.int32, (BLOCK, LANES), 1)
    low = lane < HEAD_DIM
    if has_sink:
        assert n_back < BLOCK
        sink_col = ki == 0
        not_row0 = lax.broadcasted_iota(jnp.int32, (2 * BLOCK, HEAD_DIM), 0) > 0
    zero = jnp.zeros((2 * BLOCK, HEAD_DIM), BF16)
    one = jnp.ones((2 * BLOCK, HEAD_DIM), BF16)
    nt = (((1,), (1,)), ((), ()))
    npair = GQA // 2
    for sb in range(qb):
        rows = slice(sb * BLOCK, (sb + 1) * BLOCK)
        q = q_ref[rows, :]
        kk = k_all[sb * BLOCK:(sb + 2) * BLOCK]
        vv = v_all[sb * BLOCK:(sb + 2) * BLOCK]
        valid = in_band & ((ki >= BLOCK) | (i > 0)) if sb == 0 else in_band
        _band_block(q, kk, vv, valid, sink_ref, o_ref.at[rows], lse_ref.at[rows] if with_lse else None,
                    kvh=kvh, consts=(zero, one, nt, npair, lane, low, sink_col if has_sink else None,
                                     not_row0 if has_sink else None))


def _band_block(q, kk, vv, valid, sink_ref, o_ref, lse_ref, *, kvh, consts):
    zero, one, nt, npair, lane, low, sink_col, not_row0 = consts
    has_sink = sink_ref is not None
    with_lse = lse_ref is not None
    lse_tile = jnp.zeros((BLOCK, LANES), F32)
    scores = []
    for h in range(kvh):
        k_h = kk[:, h * HEAD_DIM:(h + 1) * HEAD_DIM]
        k_sel = (jnp.concatenate([k_h, zero], axis=1), jnp.concatenate([zero, k_h], axis=1))
        q_pairs = jnp.concatenate(
            [q[:, (h * GQA + 2 * j) * HEAD_DIM:(h * GQA + 2 * j + 2) * HEAD_DIM] for j in range(npair)], axis=0)
        scores.append([lax.dot_general(q_pairs, k_sel[par], nt, preferred_element_type=F32) for par in range(2)])
    maxes, probs = [], []
    for h in range(kvh):
        maxes.append([])
        probs.append([])
        for par in range(2):
            es = []
            for j in range(npair):
                fill = NEG_INF
                if has_sink:
                    fill = jnp.where(sink_col, sink_ref[h * GQA + 2 * j + par], NEG_INF)
                s_g = jnp.where(valid, scores[h][par][j * BLOCK:(j + 1) * BLOCK], fill)
                m = jnp.max(s_g, axis=-1, keepdims=True)
                es.append(jnp.exp(s_g - m).astype(BF16))
                maxes[h].append(m)
            probs[h].append(jnp.concatenate(es, axis=0))
    results = []
    for h in range(kvh):
        v_h = vv[:, h * HEAD_DIM:(h + 1) * HEAD_DIM]
        if has_sink:
            v_h = jnp.where(not_row0, v_h, zero)
        v_sel = (jnp.concatenate([v_h, zero, one, zero], axis=1),
                 jnp.concatenate([zero, v_h, zero, one], axis=1))
        results.append(jnp.dot(probs[h][0], v_sel[0], preferred_element_type=F32)
                       + jnp.dot(probs[h][1], v_sel[1], preferred_element_type=F32))
    for h in range(kvh):
        for j in range(npair):
            blk = results[h][j * BLOCK:(j + 1) * BLOCK]
            den = blk[:, LANES:]
            col = (h * GQA + 2 * j) * HEAD_DIM
            o_ref[:, col:col + LANES] = (blk[:, :LANES] * (1.0 / den)).astype(o_ref.dtype)
            if with_lse:
                m_pair = jnp.where(low, maxes[h][j], maxes[h][npair + j])
                pair_id = h * npair + j
                lse_tile = jnp.where(_mod(lane, HEAD_DIM) == pair_id, jnp.log(den) + m_pair, lse_tile)
    if with_lse:
        lse_ref[...] = lse_tile


def _band_attn(q, k, v, n_back, sinks=None, with_lse=False):
    n, length, c = q.shape
    ck = k.shape[2]
    kvh = ck // HEAD_DIM
    assert c == kvh * GQA * HEAD_DIM and length % BLOCK == 0 and n_back <= BLOCK
    qb = 2 if length % (2 * BLOCK) == 0 else 1
    nb = length // (qb * BLOCK)
    cur = lambda b, i: (b, i, 0)
    prev = lambda b, i: (b, jnp.maximum(qb * i - 1, 0), 0)
    in_specs = [
        pl.BlockSpec((None, qb * BLOCK, c), cur),
        pl.BlockSpec((None, BLOCK, ck), prev), pl.BlockSpec((None, qb * BLOCK, ck), cur),
        pl.BlockSpec((None, BLOCK, ck), prev), pl.BlockSpec((None, qb * BLOCK, ck), cur),
    ]
    args = [q, k, k, v, v]
    if sinks is not None:
        in_specs = [pl.BlockSpec(memory_space=pltpu.SMEM)] + in_specs
        args = [sinks.astype(F32)] + args
    out_specs = [pl.BlockSpec((None, qb * BLOCK, c), cur)]
    out_shape = [jax.ShapeDtypeStruct(q.shape, BF16)]
    if with_lse:
        out_specs.append(pl.BlockSpec((None, qb * BLOCK, LANES), cur))
        out_shape.append(jax.ShapeDtypeStruct((n, length, LANES), F32))
    res = pl.pallas_call(
        functools.partial(_band_kernel, n_back=n_back, kvh=kvh, has_sink=sinks is not None, with_lse=with_lse,
                          qb=qb),
        grid=(n, nb),
        in_specs=in_specs,
        out_specs=out_specs,
        out_shape=out_shape,
        compiler_params=_params("parallel", "arbitrary"),
        name="band_attn",
    )(*args)
    return res if with_lse else res[0]


def _permute_kernel(x_ref, o_ref):
    d = o_ref.shape[0]
    rows = o_ref.shape[1]
    for r in range(d):
        o_ref[r] = x_ref[pl.ds(r, rows, stride=d), :].astype(o_ref.dtype)


def _by_residue(x, d, out_dtype):
    n, t, c = x.shape
    return pl.pallas_call(
        _permute_kernel,
        grid=(n, c // LANES),
        in_specs=[pl.BlockSpec((None, t, LANES), lambda b, j: (b, 0, j))],
        out_specs=pl.BlockSpec((None, d, t // d, LANES), lambda b, j: (b, 0, 0, j)),
        out_shape=jax.ShapeDtypeStruct((n, d, t // d, c), out_dtype),
        compiler_params=_params("parallel", "parallel"),
        name="by_residue",
    )(x)


def _combine_kernel(*refs, dils):
    ng = len(dils)
    o_refs, l_refs, out_ref = refs[:ng], refs[ng:2 * ng], refs[2 * ng]
    stage = list(refs[2 * ng + 1:])
    c = out_ref.shape[1]
    nchunk = c // LANES
    outs, lses = [], []
    for o_ref, l_ref, d in zip(o_refs, l_refs, dils):
        if d == 1:
            outs.append(lambda j, o_ref=o_ref: o_ref[0, :, j * LANES:(j + 1) * LANES].astype(F32))
            lses.append(l_ref[0])
            continue
        o_st, l_st = stage.pop(0), stage.pop(0)
        rows = o_ref.shape[1]
        for r in range(d):
            l_st[pl.ds(r, rows, stride=d), :] = l_ref[r]
            for j in range(nchunk):
                o_st[j, pl.ds(r, rows, stride=d), :] = o_ref[r, :, j * LANES:(j + 1) * LANES].astype(F32)
        outs.append(lambda j, o_st=o_st: o_st[j])
        lses.append(l_st[...])
    top = functools.reduce(jnp.maximum, lses)
    ws = [jnp.exp(l - top) for l in lses]
    inv = 1.0 / functools.reduce(jnp.add, ws)
    split = []
    for w in ws[:-1]:
        w = w * inv
        hi = w.astype(BF16)
        split.append(jnp.concatenate([hi, (w - hi.astype(F32)).astype(BF16)], axis=1))
    assert nchunk % 2 == 0
    for jj in range(nchunk // 2):
        head_of_col = _div(lax.broadcasted_iota(jnp.int32, (2 * LANES, 2 * LANES), 1) + jj * 2 * LANES, HEAD_DIM)
        lane_of_col = _mod(head_of_col, 2) * HEAD_DIM + _div(head_of_col, 2)
        row = _mod(lax.broadcasted_iota(jnp.int32, (2 * LANES, 2 * LANES), 0), LANES)
        expand = (lane_of_col == row).astype(BF16)
        wides = [jnp.dot(w2, expand, preferred_element_type=F32) for w2 in split]
        for half in range(2):
            j = 2 * jj + half
            cols = slice(half * LANES, (half + 1) * LANES)
            acc, others = None, None
            for o, wide in zip(outs[:-1], wides):
                term = wide[:, cols] * o(j)
                acc = term if acc is None else acc + term
                others = wide[:, cols] if others is None else others + wide[:, cols]
            acc = acc + (1.0 - others) * outs[-1](j)
            out_ref[:, j * LANES:(j + 1) * LANES] = acc.astype(out_ref.dtype)


def _combine(outs, lses, dils, tm=256):
    n, _, _, c = outs[0].shape
    t = outs[0].shape[1] * outs[0].shape[2]
    tm = _pick(t, tm)
    assert all(tm % (16 * d) == 0 for d in dils)
    blk = lambda width: [pl.BlockSpec((None, d, tm // d, width), lambda b, i: (b, 0, i, 0)) for d in dils]
    scratch = []
    for d in dils:
        if d > 1:
            scratch += [pltpu.VMEM((c // LANES, tm, LANES), F32), pltpu.VMEM((tm, LANES), F32)]
    return pl.pallas_call(
        functools.partial(_combine_kernel, dils=tuple(dils)),
        grid=(n, t // tm),
        in_specs=blk(c) + blk(LANES),
        out_specs=pl.BlockSpec((None, tm, c), lambda b, i: (b, i, 0)),
        out_shape=jax.ShapeDtypeStruct((n, t, c), BF16),
        scratch_shapes=scratch,
        compiler_params=_params("parallel", "parallel"),
        name="combine",
    )(*outs, *lses)


def _stack_heads(q, base, t):
    return jnp.concatenate([q[:, (base + g) * HEAD_DIM:(base + g + 1) * HEAD_DIM] for g in range(GQA)], axis=0)


def _sample_a_kernel(sink_ref, q_ref, kc_ref, kn_ref, vc_ref, vn_ref, o_ref, *, kvh, window):
    for b in range(q_ref.shape[0]):
        _sample_a_one(sink_ref, q_ref.at[b], kc_ref.at[b], kn_ref.at[b], vc_ref.at[b], vn_ref.at[b], o_ref.at[b],
                      kvh=kvh, window=window)


def _sample_a_one(sink_ref, q_ref, kc_ref, kn_ref, vc_ref, vn_ref, o_ref, *, kvh, window):
    t = q_ref.shape[0]
    buf = kc_ref.shape[0]
    q = q_ref[...]
    kc = jnp.concatenate([kc_ref[...], kn_ref[...]], axis=0).astype(BF16)
    vc = jnp.concatenate([vc_ref[...], vn_ref[...]], axis=0).astype(BF16)
    rows = GQA * t
    row = lax.broadcasted_iota(jnp.int32, (rows, buf + t), 0)
    col = lax.broadcasted_iota(jnp.int32, (rows, buf + t), 1)
    dist = buf + _mod(row, t) - col
    valid = (dist >= 0) & (dist < window)
    rowg = _div(lax.broadcasted_iota(jnp.int32, (rows, 1), 0), t)
    scores = [lax.dot_general(_stack_heads(q, h * GQA, t).astype(BF16), kc[:, h * HEAD_DIM:(h + 1) * HEAD_DIM],
                              (((1,), (1,)), ((), ())), preferred_element_type=F32) for h in range(kvh)]
    probs = []
    for h in range(kvh):
        s = jnp.where(valid, scores[h], NEG_INF)
        sink = jnp.zeros((rows, 1), F32)
        for g in range(GQA):
            sink = jnp.where(rowg == g, sink_ref[h * GQA + g], sink)
        m = jnp.maximum(jnp.max(s, axis=-1, keepdims=True), sink)
        e = jnp.exp(s - m)
        den = jnp.sum(e, axis=-1, keepdims=True) + jnp.exp(sink - m)
        probs.append((e.astype(BF16), 1.0 / den))
    for h in range(kvh):
        e, inv = probs[h]
        o = jnp.dot(e, vc[:, h * HEAD_DIM:(h + 1) * HEAD_DIM], preferred_element_type=F32) * inv
        for g in range(GQA):
            col0 = (h * GQA + g) * HEAD_DIM
            o_ref[:, col0:col0 + HEAD_DIM] = o[g * t:(g + 1) * t].astype(o_ref.dtype)


def _sample_a_attn(q, kcache, knew, vcache, vnew, sinks):
    n, t, c = q.shape
    buf, ck = kcache.shape[1:]
    bb = _pick(n, 4)
    row3 = lambda b: (b, 0, 0)
    return pl.pallas_call(
        functools.partial(_sample_a_kernel, kvh=ck // HEAD_DIM, window=A_WINDOW),
        grid=(n // bb,),
        in_specs=[pl.BlockSpec(memory_space=pltpu.SMEM),
                  pl.BlockSpec((bb, t, c), row3),
                  pl.BlockSpec((bb, buf, ck), row3), pl.BlockSpec((bb, t, ck), row3),
                  pl.BlockSpec((bb, buf, ck), row3), pl.BlockSpec((bb, t, ck), row3)],
        out_specs=pl.BlockSpec((bb, t, c), row3),
        out_shape=jax.ShapeDtypeStruct((n, t, c), F32),
        compiler_params=_params("parallel"),
        name="sample_a_attn",
    )(sinks.astype(F32), q, kcache, knew, vcache, vnew)


def _sample_b_kernel(q_ref, kt_ref, kn_ref, vt_ref, vn_ref, o_ref, *, kvh):
    t = q_ref.shape[0]
    buf = kt_ref.shape[1]
    c = o_ref.shape[1]
    rows = GQA * t
    q = q_ref[...]
    kn = kn_ref[...].astype(BF16)
    vn = vn_ref[...].astype(BF16)
    tok_n = _mod(lax.broadcasted_iota(jnp.int32, (rows, t), 0), t)
    dist_n = tok_n - lax.broadcasted_iota(jnp.int32, (rows, t), 1)
    ones_c = jnp.ones((HEAD_DIM, buf), BF16)
    ones_n = jnp.ones((t, HEAD_DIM), BF16)
    nt = (((1,), (1,)), ((), ()))
    starts = [max(buf - window, 0) for window, _ in B_GROUPS]
    pairs = [(h, gi) for h in range(kvh) for gi in range(len(B_GROUPS))]
    hsl = lambda h: slice(h * HEAD_DIM, (h + 1) * HEAD_DIM)
    scores = []
    for h, gi in pairs:
        if gi == 0:
            kt_h = kt_ref[hsl(h), :].astype(BF16)
        q_h = _stack_heads(q, gi * (c // HEAD_DIM) + h * GQA, t).astype(BF16)
        scores.append((jnp.dot(q_h, kt_h[:, starts[gi]:], preferred_element_type=F32),
                       lax.dot_general(q_h, kn[:, hsl(h)], nt, preferred_element_type=F32)))
    probs = []
    for (h, gi), (s_c, s_n) in zip(pairs, scores):
        window, d = B_GROUPS[gi]
        span = buf - starts[gi]
        tok = _mod(lax.broadcasted_iota(jnp.int32, (rows, span), 0), t)
        dist_c = buf + tok - (starts[gi] + lax.broadcasted_iota(jnp.int32, (rows, span), 1))
        s_c = jnp.where((_mod(dist_c, d) == 0) & (dist_c <= window), s_c, NEG_INF)
        s_n = jnp.where((dist_n >= 0) & (_mod(dist_n, d) == 0) & (dist_n <= window), s_n, NEG_INF)
        m = jnp.maximum(jnp.max(s_c, axis=-1, keepdims=True), jnp.max(s_n, axis=-1, keepdims=True))
        probs.append((jnp.exp(s_c - m).astype(BF16), jnp.exp(s_n - m).astype(BF16), m))
    results = []
    for (h, gi), (e_c, e_n, m) in zip(pairs, probs):
        if gi == 0:
            vt_h = jnp.concatenate([vt_ref[hsl(h), :].astype(BF16), ones_c, ones_c], axis=0)
            vn_h = jnp.concatenate([vn[:, hsl(h)], ones_n, ones_n], axis=1)
        res = (lax.dot_general(e_c, vt_h[:, starts[gi]:], nt, preferred_element_type=F32)
               + jnp.dot(e_n, vn_h, preferred_element_type=F32))
        results.append((res, m))
    for h in range(kvh):
        outs, lses = [], []
        for res, m in results[h * len(B_GROUPS):(h + 1) * len(B_GROUPS)]:
            den = res[:, LANES:LANES + HEAD_DIM]
            outs.append(res[:, :HEAD_DIM] * (1.0 / den))
            lses.append(jnp.log(den) + m)
        top = functools.reduce(jnp.maximum, lses)
        ws = [jnp.exp(l - top) for l in lses]
        inv = 1.0 / functools.reduce(jnp.add, ws)
        o = functools.reduce(jnp.add, [o_g * (w * inv) for o_g, w in zip(outs, ws)])
        for g in range(GQA):
            col0 = (h * GQA + g) * HEAD_DIM
            o_ref[:, col0:col0 + HEAD_DIM] = o[g * t:(g + 1) * t].astype(o_ref.dtype)


def _sample_b_attn(q, kt_cache, knew, vt_cache, vnew):
    n, t, gc = q.shape
    c = gc // len(B_GROUPS)
    ck, buf = kt_cache.shape[1:]
    assert all(max(buf - w, 0) % LANES == 0 for w, _ in B_GROUPS)
    row3 = lambda b: (b, 0, 0)
    return pl.pallas_call(
        functools.partial(_sample_b_kernel, kvh=ck // HEAD_DIM),
        grid=(n,),
        in_specs=[pl.BlockSpec((None, t, gc), row3),
                  pl.BlockSpec((None, ck, buf), row3), pl.BlockSpec((None, t, ck), row3),
                  pl.BlockSpec((None, ck, buf), row3), pl.BlockSpec((None, t, ck), row3)],
        out_specs=pl.BlockSpec((None, t, c), row3),
        out_shape=jax.ShapeDtypeStruct((n, t, c), F32),
        compiler_params=_params("parallel"),
        name="sample_b_attn",
    )(q, kt_cache, knew, vt_cache, vnew)


def _rope_tables(pos):
    half = HEAD_DIM // 2
    inv = ROPE_THETA ** (-jnp.arange(half, dtype=F32) / half)
    ang = pos.astype(F32)[:, None] * inv[None, :]
    cos, sin = jnp.cos(ang), jnp.sin(ang)
    reps = LANES // HEAD_DIM
    return jnp.tile(jnp.concatenate([cos, cos], axis=1), (1, reps)), jnp.tile(jnp.concatenate([-sin, sin], axis=1), (1, reps))


def _trunk(x, pos, sample_caches, wts, w16):
    (ln_g, ln_b, w_qkv_a, sinks_a, w_o_a, w_kv_b, w_q_b, w_o_b, w_up, w_down) = wts
    n, t, dm = x.shape
    m = n * t
    depth = ln_g.shape[0]
    n_a = w_qkv_a.shape[0]
    alpha = (2 * depth) ** 0.25
    qa = w_o_a.shape[1]
    kva = (w_qkv_a.shape[2] - qa) // 2
    kvb = w_kv_b.shape[1] // 2
    cb = w_o_b.shape[1]
    ng = len(B_GROUPS)
    scale = HEAD_DIM ** -0.5
    is_prompt = sample_caches is None
    rope = _rope_tables(pos)
    if is_prompt:
        rope_by_d = {d: rope if d == 1 else _rope_tables(pos.reshape(t // d, d).T.reshape(t)) for _, d in B_GROUPS}
    if not is_prompt:
        rope = tuple(jnp.tile(r, (n, 1)) for r in rope)
    qdt = BF16 if is_prompt else F32

    def project(key, xin, w, layer, col0, ncols, out_dtype, side=None, side_key=None, **kw):
        if is_prompt:
            res = _proj(xin, w, out_dtype, layer=layer, col0=col0, n=ncols, side=side, **kw)
            w16[key] = res[1]
            if side is not None:
                w16[side_key] = res[2]
            return res[0]
        return _proj(xin, w16[key], out_dtype, **kw)

    x32 = x.reshape(m, dm)
    x16 = x32.astype(BF16)
    a_states = []
    b_state = None
    b_kv = None
    for layer in range(depth):
        if layer < n_a:
            q = project(("q_a", layer), x16, w_qkv_a, layer, 0, qa, qdt, rope=rope, rope_cols=qa, scale=scale,
                        side=(w_o_a, layer), side_key=("o_a", layer))
            kv = project(("kv_a", layer), x16, w_qkv_a, layer, qa, 2 * kva, F32, rope=rope, rope_cols=kva)
            k, v = kv[:, :kva], kv[:, kva:]
            if is_prompt:
                k3, v3 = k.reshape(n, t, kva), v.reshape(n, t, kva)
                o = _band_attn(q.reshape(n, t, qa), k3, v3, A_WINDOW - 1, sinks=sinks_a[layer]).reshape(m, qa)
                keep = min(A_WINDOW, t)
                a_states.append((k3[:, -keep:], v3[:, -keep:]))
            else:
                ck = sample_caches[0][layer].reshape(n, -1, kva)
                cv = sample_caches[1][layer].reshape(n, -1, kva)
                k3, v3 = k.reshape(n, t, kva), v.reshape(n, t, kva)
                o = _sample_a_attn(q.reshape(n, t, qa), ck, k3, cv, v3, sinks_a[layer]).reshape(m, qa)
                keep = ck.shape[1]
                a_states.append((jnp.concatenate([ck, k3], axis=1)[:, -keep:],
                                 jnp.concatenate([cv, v3], axis=1)[:, -keep:]))
            x32, x16 = _res_ln(o, w16[("o_a", layer)], x32, ln_g[layer, 0], ln_b[layer, 0], alpha)
        else:
            j = layer - n_a
            if j == 0:
                kv = project("kv_b", x16, w_kv_b, None, 0, 2 * kvb, F32, rope=rope, rope_cols=kvb)
                k3, v3 = kv[:, :kvb].reshape(n, t, kvb), kv[:, kvb:].reshape(n, t, kvb)
                if is_prompt:
                    keep = min(max(w for w, _ in B_GROUPS), t)
                    b_state = (k3[:, -keep:], v3[:, -keep:])
                    b_kv = [(k3, v3) if d == 1 else
                            tuple(_by_residue(a, d, F32).reshape(n * d, t // d, kvb) for a in (k3, v3))
                            for _, d in B_GROUPS]
                else:
                    ck = sample_caches[2].reshape(n, -1, kvb)
                    cv = sample_caches[3].reshape(n, -1, kvb)
                    keep = ck.shape[1]
                    b_state = (jnp.concatenate([ck, k3], axis=1)[:, -keep:],
                               jnp.concatenate([cv, v3], axis=1)[:, -keep:])
                    b_kv = (jnp.swapaxes(ck, 1, 2), k3, jnp.swapaxes(cv, 1, 2), v3)
            if is_prompt:
                outs, lses = [], []
                for gi, (window, d) in enumerate(B_GROUPS):
                    xg = x16 if d == 1 else _by_residue(x32.reshape(n, t, dm), d, BF16).reshape(m, dm)
                    side = dict(side=(w_o_b, j), side_key=("o_b", j)) if gi == 0 else {}
                    q = project(("q_b", j, gi), xg, w_q_b, j, gi * cb, cb, BF16,
                                rope=rope_by_d[d], rope_cols=cb, scale=scale, **side)
                    o_g, lse_g = _band_attn(q.reshape(n * d, t // d, cb), b_kv[gi][0], b_kv[gi][1],
                                            window // d, with_lse=True)
                    outs.append(o_g.reshape(n, d, t // d, cb))
                    lses.append(lse_g.reshape(n, d, t // d, LANES))
                o = _combine(outs, lses, [d for _, d in B_GROUPS]).reshape(m, cb)
            else:
                q = jnp.concatenate([project(("q_b", j, gi), x16, None, None, 0, cb, qdt,
                                             rope=rope, rope_cols=cb, scale=scale) for gi in range(ng)], axis=1)
                o = _sample_b_attn(q.reshape(n, t, ng * cb), *b_kv).reshape(m, cb)
            x32, x16 = _res_ln(o, w16[("o_b", j)], x32, ln_g[layer, 0], ln_b[layer, 0], alpha)
        hid = project(("up", layer), x16, w_up, layer, 0, w_up.shape[2], BF16, relu2=True,
                      side=(w_down, layer), side_key=("down", layer))
        x32, x16 = _res_ln(hid, w16[("down", layer)], x32, ln_g[layer, 1], ln_b[layer, 1], alpha)
    return x32.reshape(n, t, dm), a_states, b_state


def kernel(x_prompt, x_sample, cache_a_k, cache_a_v, cache_b_k, cache_b_v, ln_g, ln_b,
           w_qkv_a, sinks_a, w_o_a, w_kv_b, w_q_b, w_o_b, w_up, w_down):
    wts = (ln_g, ln_b, w_qkv_a, sinks_a, w_o_a, w_kv_b, w_q_b, w_o_b, w_up, w_down)
    w16 = {}
    kv_heads_a = cache_a_k.shape[3]
    kv_heads_b = cache_b_k.shape[2]
    pos_prompt = jnp.arange(x_prompt.shape[1], dtype=jnp.int32)
    pos_sample = PAST_LEN + jnp.arange(x_sample.shape[1], dtype=jnp.int32)
    y_prompt, a_prompt, b_prompt = _trunk(x_prompt, pos_prompt, None, wts, w16)
    y_sample, a_sample, b_sample = _trunk(x_sample, pos_sample, (cache_a_k, cache_a_v, cache_b_k, cache_b_v), wts, w16)

    def heads(s, kvh):
        return s.reshape(*s.shape[:-1], kvh, HEAD_DIM)

    return (y_prompt, y_sample,
            jnp.stack([heads(s[0], kv_heads_a) for s in a_prompt], axis=0),
            jnp.stack([heads(s[1], kv_heads_a) for s in a_prompt], axis=0),
            jnp.stack([heads(s[0], kv_heads_a) for s in a_sample], axis=0),
            jnp.stack([heads(s[1], kv_heads_a) for s in a_sample], axis=0),
            heads(b_prompt[0], kv_heads_b), heads(b_prompt[1], kv_heads_b),
            heads(b_sample[0], kv_heads_b), heads(b_sample[1], kv_heads_b))
```

```python
import functools

import jax
import jax.numpy as jnp
from jax import lax
from jax.experimental import pallas as pl
from jax.experimental.pallas import tpu as pltpu

HEAD_DIM = 64
GQA = 8
A_WINDOW = 128
B_GROUPS = ((128, 1), (512, 4), (2048, 16))
PAST_LEN = 16384
ROPE_THETA = 10000.0
LN_EPS = 1e-5
BLOCK = 128
NEG_INF = -1e30
LANES = 128
VMEM_LIMIT = 56 * 1024 * 1024

LN_ROWS = 16

BF16 = jnp.bfloat16
F32 = jnp.float32


def _mod(x, n):
    assert n & (n - 1) == 0, "power-of-two modulus only"
    return x & (n - 1)


def _div(x, n):
    assert n & (n - 1) == 0, "power-of-two divisor only"
    return x >> (n.bit_length() - 1)


def _params(*sem):
    return pltpu.CompilerParams(dimension_semantics=sem, vmem_limit_bytes=VMEM_LIMIT)


def _pick(n, pref):
    if n <= pref:
        return n
    t = pref
    while n % t:
        t //= 2
    return t


def _rope_chunk(x, cos, sin_signed):
    lane = lax.broadcasted_iota(jnp.int32, x.shape, 1)
    first = _mod(lane, HEAD_DIM) < (HEAD_DIM // 2)
    partner = jnp.where(first, pltpu.roll(x, LANES - HEAD_DIM // 2, 1), pltpu.roll(x, HEAD_DIM // 2, 1))
    return x * cos + partner * sin_signed


def _proj_kernel(*refs, rope_chunks, scale, relu2, cast_w, has_side, delay_tiles):
    refs = list(refs)
    a_ref, w_ref = refs[:2]
    del refs[:2]
    cos_ref, sin_ref = (refs.pop(0), refs.pop(0)) if rope_chunks else (None, None)
    if has_side:
        side_ref = refs.pop(0)
    o_ref = refs.pop(0)
    if has_side:
        side_out_ref = refs.pop(1 if cast_w else 0)
        side_out_ref[...] = side_ref[...].astype(BF16)
    i = pl.program_id(1)
    if cast_w:
        w16_out_ref, w16_ref = refs[:2]

        @pl.when(i == 0)
        def _():
            w16 = w_ref[...].astype(BF16)
            w16_ref[...] = w16
            w16_out_ref[...] = w16

    def product():
        w = w16_ref[...] if cast_w else w_ref[...]
        return jnp.dot(a_ref[...].astype(BF16), w, preferred_element_type=F32)

    def finish(acc):
        cos = cos_ref[...]
        sin = sin_ref[...]
        for c in range(acc.shape[1] // LANES):
            x = acc[:, c * LANES:(c + 1) * LANES]
            if c < rope_chunks:
                x = _rope_chunk(x, cos, sin)
                if scale != 1.0:
                    x = x * scale
            o_ref[:, c * LANES:(c + 1) * LANES] = x.astype(o_ref.dtype)

    if delay_tiles:
        accs = refs[-2:]

        @pl.when(i == 0)
        def _():
            accs[0][...] = product()

        for parity in range(2):
            @pl.when((i > 0) & (i < delay_tiles) & (i % 2 == parity))
            def _():
                accs[parity][...] = product()
                finish(accs[1 - parity])

        @pl.when(i == delay_tiles)
        def _():
            finish(accs[(delay_tiles - 1) % 2])

        return
    acc = product()
    if relu2:
        acc = jnp.square(jnp.maximum(acc, 0.0))
    if not rope_chunks:
        o_ref[...] = acc.astype(o_ref.dtype)
        return
    finish(acc)


def _proj(a, w, out_dtype, *, layer=None, col0=0, n=None, rope=None, rope_cols=0, scale=1.0, relu2=False,
          side=None, tm=1024, tn=1024):
    m, k = a.shape
    n = w.shape[-1] if n is None else n
    if m <= tm // 2:
        tn *= 2
    elif a.dtype != BF16:
        tm //= 2
    tm = _pick(m, tm)
    tn = _pick(n, tn)
    assert col0 % tn == 0
    joff = col0 // tn
    cast_w = w.dtype != BF16
    rope_chunks = 0
    if rope is not None:
        cos, sin = rope
        p = cos.shape[0]
        tm = _pick(p, tm)
        pb = p // tm
        assert rope_cols == n or tn == n
        rope_chunks = min(rope_cols, tn) // LANES
    nj, ni = n // tn, m // tm
    delay = rope is not None and rope_chunks * LANES == tn and ni >= 4
    row = (lambda i: jnp.minimum(i, ni - 1)) if delay else (lambda i: i)
    done = (lambda i: jnp.maximum(i - 1, 0)) if delay else (lambda i: i)
    w_mode = dict(pipeline_mode=pl.Buffered(1)) if delay and cast_w else {}
    if w.ndim == 3:
        w_spec = pl.BlockSpec((None, k, tn), lambda j, i: (layer, 0, j + joff), **w_mode)
    else:
        w_spec = pl.BlockSpec((k, tn), lambda j, i: (0, j + joff), **w_mode)
    in_specs = [pl.BlockSpec((tm, k), lambda j, i: (row(i), 0)), w_spec]
    args = [a, w]
    if rope is not None:
        in_specs += [pl.BlockSpec((tm, LANES), lambda j, i: (done(i) % pb, 0))] * 2
        args += [cos, sin]
    out_specs = [pl.BlockSpec((tm, tn), lambda j, i: (done(i), j))]
    out_shape = [jax.ShapeDtypeStruct((m, n), out_dtype)]
    scratch = []
    if cast_w:
        out_specs.append(pl.BlockSpec((k, tn), lambda j, i: (0, j)))
        out_shape.append(jax.ShapeDtypeStruct((k, n), BF16))
        scratch.append(pltpu.VMEM((k, tn), BF16))
    if delay:
        scratch += [pltpu.VMEM((tm, tn), F32), pltpu.VMEM((tm, tn), F32)]
    if side is not None:
        w_side, side_layer = side
        _, rows, cols = w_side.shape
        rs = rows // (nj * ni)
        assert rs * nj * ni == rows and rs % 16 == 0
        in_specs.append(pl.BlockSpec((None, rs, cols), lambda j, i: (side_layer, j * ni + row(i), 0)))
        args.append(w_side)
        out_specs.append(pl.BlockSpec((rs, cols), lambda j, i: (j * ni + row(i), 0)))
        out_shape.append(jax.ShapeDtypeStruct((rows, cols), BF16))
    res = pl.pallas_call(
        functools.partial(_proj_kernel, rope_chunks=rope_chunks, scale=scale, relu2=relu2, cast_w=cast_w,
                          has_side=side is not None, delay_tiles=ni if delay else 0),
        grid=(nj, ni + 1 if delay else ni),
        in_specs=in_specs,
        out_specs=out_specs,
        out_shape=out_shape,
        scratch_shapes=scratch,
        compiler_params=_params("parallel", "arbitrary"),
        name="proj",
    )(*args)
    return tuple(res) if len(res) > 1 else res[0]


def _res_ln_kernel(a_ref, w_ref, x_ref, g_ref, b_ref, o32_ref, o16_ref, acc0_ref, acc1_ref, *, alpha, mt, nk):
    i = pl.program_id(0)
    kk = pl.program_id(1)
    accs = (acc0_ref, acc1_ref)
    rc = acc0_ref.shape[0] // nk

    def matmul(acc_ref):
        acc_ref[...] += jnp.dot(a_ref[...].astype(BF16), w_ref[...], preferred_element_type=F32)

    def layernorm(acc_ref):
        gain = g_ref[...]
        bias = b_ref[...]
        for s in range(rc // LN_ROWS):
            rows = pl.ds(pl.multiple_of(kk * rc + s * LN_ROWS, LN_ROWS), LN_ROWS)
            v = alpha * x_ref[rows, :] + acc_ref[rows, :]
            acc_ref[rows, :] = jnp.zeros((LN_ROWS, acc_ref.shape[1]), F32)
            mu = jnp.mean(v, axis=-1, keepdims=True)
            c = v - mu
            var = jnp.mean(c * c, axis=-1, keepdims=True)
            y = c * lax.rsqrt(var + LN_EPS) * gain + bias
            o32_ref[rows, :] = y
            o16_ref[rows, :] = y.astype(BF16)

    @pl.when((i == 0) & (kk == 0))
    def _():
        acc0_ref[...] = jnp.zeros_like(acc0_ref)
        acc1_ref[...] = jnp.zeros_like(acc1_ref)

    @pl.when(i == 0)
    def _():
        matmul(acc0_ref)

    for parity in range(2):
        @pl.when((i > 0) & (i < mt) & (i % 2 == parity))
        def _():
            matmul(accs[parity])
            layernorm(accs[1 - parity])

    @pl.when(i == mt)
    def _():
        layernorm(accs[(mt - 1) % 2])


def _res_ln(a, w, x, g, b, alpha, *, tm=512, tk=2048):
    m, k = a.shape
    n = w.shape[1]
    tm = _pick(m, tm)
    tk = _pick(k, tk)
    while tk < k and (tm // (k // tk)) % LN_ROWS:
        tk *= 2
    mt, nk = m // tm, k // tk
    assert (tm // nk) % LN_ROWS == 0 and tm % nk == 0
    a_map = lambda i, kk: (jnp.minimum(i, mt - 1), jnp.where(i < mt, kk, nk - 1))
    w_map = lambda i, kk: (jnp.where(i < mt, kk, nk - 1), 0)
    prev_map = lambda i, kk: (jnp.maximum(i - 1, 0), 0)
    return pl.pallas_call(
        functools.partial(_res_ln_kernel, alpha=alpha, mt=mt, nk=nk),
        grid=(mt + 1, nk),
        in_specs=[
            pl.BlockSpec((tm, tk), a_map),
            pl.BlockSpec((tk, n), w_map),
            pl.BlockSpec((tm, n), prev_map),
            pl.BlockSpec((1, n), lambda i, kk: (0, 0)),
            pl.BlockSpec((1, n), lambda i, kk: (0, 0)),
        ],
        out_specs=[pl.BlockSpec((tm, n), prev_map), pl.BlockSpec((tm, n), prev_map)],
        out_shape=[jax.ShapeDtypeStruct((m, n), F32), jax.ShapeDtypeStruct((m, n), BF16)],
        scratch_shapes=[pltpu.VMEM((tm, n), F32), pltpu.VMEM((tm, n), F32)],
        compiler_params=_params("arbitrary", "arbitrary"),
        name="res_ln",
    )(a, w, x, g.reshape(1, n), b.reshape(1, n))


def _band_kernel(*refs, n_back, kvh, has_sink, with_lse, qb):
    refs = list(refs)
    sink_ref = refs.pop(0) if has_sink else None
    q_ref, kp_ref, kc_ref, vp_ref, vc_ref, o_ref = refs[:6]
    lse_ref = refs[6] if with_lse else None
    i = pl.program_id(1)
    k_all = jnp.concatenate([kp_ref[...], kc_ref[...]], axis=0).astype(BF16)
    v_all = jnp.concatenate([vp_ref[...], vc_ref[...]], axis=0).astype(BF16)
    qi = lax.broadcasted_iota(jnp.int32, (BLOCK, 2 * BLOCK), 0)
    ki = lax.broadcasted_iota(jnp.int32, (BLOCK, 2 * BLOCK), 1)
    dist = BLOCK + qi - ki
    in_band = (dist >= 0) & (dist <= n_back)
    lane = lax.broadcasted_iota(jnp.int32, (BLOCK, LANES), 1)
    low = lane < HEAD_DIM
    if has_sink:
        assert n_back < BLOCK
        sink_col = ki == 0
        not_row0 = lax.broadcasted_iota(jnp.int32, (2 * BLOCK, HEAD_DIM), 0) > 0
    zero = jnp.zeros((2 * BLOCK, HEAD_DIM), BF16)
    one = jnp.ones((2 * BLOCK, HEAD_DIM), BF16)
    nt = (((1,), (1,)), ((), ()))
    npair = GQA // 2
    for sb in range(qb):
        rows = slice(sb * BLOCK, (sb + 1) * BLOCK)
        q = q_ref[rows, :]
        kk = k_all[sb * BLOCK:(sb + 2) * BLOCK]
        vv = v_all[sb * BLOCK:(sb + 2) * BLOCK]
        valid = in_band & ((ki >= BLOCK) | (i > 0)) if sb == 0 else in_band
        _band_block(q, kk, vv, valid, sink_ref, o_ref.at[rows], lse_ref.at[rows] if with_lse else None,
                    kvh=kvh, consts=(zero, one, nt, npair, lane, low, sink_col if has_sink else None,
                                     not_row0 if has_sink else None))


def _band_block(q, kk, vv, valid, sink_ref, o_ref, lse_ref, *, kvh, consts):
    zero, one, nt, npair, lane, low, sink_col, not_row0 = consts
    has_sink = sink_ref is not None
    with_lse = lse_ref is not None
    lse_tile = jnp.zeros((BLOCK, LANES), F32)
    scores = []
    for h in range(kvh):
        k_h = kk[:, h * HEAD_DIM:(h + 1) * HEAD_DIM]
        k_sel = (jnp.concatenate([k_h, zero], axis=1), jnp.concatenate([zero, k_h], axis=1))
        q_pairs = jnp.concatenate(
            [q[:, (h * GQA + 2 * j) * HEAD_DIM:(h * GQA + 2 * j + 2) * HEAD_DIM] for j in range(npair)], axis=0)
        scores.append([lax.dot_general(q_pairs, k_sel[par], nt, preferred_element_type=F32) for par in range(2)])
    maxes, probs = [], []
    for h in range(kvh):
        maxes.append([])
        probs.append([])
        for par in range(2):
            es = []
            for j in range(npair):
                fill = NEG_INF
                if has_sink:
                    fill = jnp.where(sink_col, sink_ref[h * GQA + 2 * j + par], NEG_INF)
                s_g = jnp.where(valid, scores[h][par][j * BLOCK:(j + 1) * BLOCK], fill)
                m = jnp.max(s_g, axis=-1, keepdims=True)
                es.append(jnp.exp(s_g - m).astype(BF16))
                maxes[h].append(m)
            probs[h].append(jnp.concatenate(es, axis=0))
    results = []
    for h in range(kvh):
        v_h = vv[:, h * HEAD_DIM:(h + 1) * HEAD_DIM]
        if has_sink:
            v_h = jnp.where(not_row0, v_h, zero)
        v_sel = (jnp.concatenate([v_h, zero, one, zero], axis=1),
                 jnp.concatenate([zero, v_h, zero, one], axis=1))
        results.append(jnp.dot(probs[h][0], v_sel[0], preferred_element_type=F32)
                       + jnp.dot(probs[h][1], v_sel[1], preferred_element_type=F32))
    for h in range(kvh):
        for j in range(npair):
            blk = results[h][j * BLOCK:(j + 1) * BLOCK]
            den = blk[:, LANES:]
            col = (h * GQA + 2 * j) * HEAD_DIM
            o_ref[:, col:col + LANES] = (blk[:, :LANES] * (1.0 / den)).astype(o_ref.dtype)
            if with_lse:
                m_pair = jnp.where(low, maxes[h][j], maxes[h][npair + j])
                pair_id = h * npair + j
                lse_tile = jnp.where(_mod(lane, HEAD_DIM) == pair_id, jnp.log(den) + m_pair, lse_tile)
    if with_lse:
        lse_ref[...] = lse_tile


def _band_attn(q, k, v, n_back, sinks=None, with_lse=False):
    n, length, c = q.shape
    ck = k.shape[2]
    kvh = ck // HEAD_DIM
    assert c == kvh * GQA * HEAD_DIM and length % BLOCK == 0 and n_back <= BLOCK
    qb = 2 if length % (2 * BLOCK) == 0 else 1
    nb = length // (qb * BLOCK)
    cur = lambda b, i: (b, i, 0)
    prev = lambda b, i: (b, jnp.maximum(qb * i - 1, 0), 0)
    in_specs = [
        pl.BlockSpec((None, qb * BLOCK, c), cur),
        pl.BlockSpec((None, BLOCK, ck), prev), pl.BlockSpec((None, qb * BLOCK, ck), cur),
        pl.BlockSpec((None, BLOCK, ck), prev), pl.BlockSpec((None, qb * BLOCK, ck), cur),
    ]
    args = [q, k, k, v, v]
    if sinks is not None:
        in_specs = [pl.BlockSpec(memory_space=pltpu.SMEM)] + in_specs
        args = [sinks.astype(F32)] + args
    out_specs = [pl.BlockSpec((None, qb * BLOCK, c), cur)]
    out_shape = [jax.ShapeDtypeStruct(q.shape, BF16)]
    if with_lse:
        out_specs.append(pl.BlockSpec((None, qb * BLOCK, LANES), cur))
        out_shape.append(jax.ShapeDtypeStruct((n, length, LANES), F32))
    res = pl.pallas_call(
        functools.partial(_band_kernel, n_back=n_back, kvh=kvh, has_sink=sinks is not None, with_lse=with_lse,
                          qb=qb),
        grid=(n, nb),
        in_specs=in_specs,
        out_specs=out_specs,
        out_shape=out_shape,
        compiler_params=_params("parallel", "arbitrary"),
        name="band_attn",
    )(*args)
    return res if with_lse else res[0]


def _permute_kernel(x_ref, *o_refs):
    for o_ref in o_refs:
        d = o_ref.shape[0]
        rows = o_ref.shape[1]
        for r in range(d):
            o_ref[r] = x_ref[pl.ds(r, rows, stride=d), :].astype(o_ref.dtype)


def _by_residue(x, dils, out_dtype):
    n, t, c = x.shape
    return pl.pallas_call(
        _permute_kernel,
        grid=(n, c // LANES),
        in_specs=[pl.BlockSpec((None, t, LANES), lambda b, j: (b, 0, j))],
        out_specs=[pl.BlockSpec((None, d, t // d, LANES), lambda b, j: (b, 0, 0, j)) for d in dils],
        out_shape=[jax.ShapeDtypeStruct((n, d, t // d, c), out_dtype) for d in dils],
        compiler_params=_params("parallel", "parallel"),
        name="by_residue",
    )(x)


def _combine_kernel(*refs, dils):
    ng = len(dils)
    o_refs, l_refs, out_ref = refs[:ng], refs[ng:2 * ng], refs[2 * ng]
    stage = list(refs[2 * ng + 1:])
    c = out_ref.shape[1]
    nchunk = c // LANES
    outs, lses = [], []
    for o_ref, l_ref, d in zip(o_refs, l_refs, dils):
        if d == 1:
            outs.append(lambda j, o_ref=o_ref: o_ref[0, :, j * LANES:(j + 1) * LANES].astype(F32))
            lses.append(l_ref[0])
            continue
        o_st, l_st = stage.pop(0), stage.pop(0)
        rows = o_ref.shape[1]
        for r in range(d):
            l_st[pl.ds(r, rows, stride=d), :] = l_ref[r]
            for j in range(nchunk):
                o_st[j, pl.ds(r, rows, stride=d), :] = o_ref[r, :, j * LANES:(j + 1) * LANES].astype(F32)
        outs.append(lambda j, o_st=o_st: o_st[j])
        lses.append(l_st[...])
    top = functools.reduce(jnp.maximum, lses)
    ws = [jnp.exp(l - top) for l in lses]
    inv = 1.0 / functools.reduce(jnp.add, ws)
    split = []
    for w in ws[:-1]:
        w = w * inv
        hi = w.astype(BF16)
        split.append(jnp.concatenate([hi, (w - hi.astype(F32)).astype(BF16)], axis=1))
    assert nchunk % 2 == 0
    for jj in range(nchunk // 2):
        head_of_col = _div(lax.broadcasted_iota(jnp.int32, (2 * LANES, 2 * LANES), 1) + jj * 2 * LANES, HEAD_DIM)
        lane_of_col = _mod(head_of_col, 2) * HEAD_DIM + _div(head_of_col, 2)
        row = _mod(lax.broadcasted_iota(jnp.int32, (2 * LANES, 2 * LANES), 0), LANES)
        expand = (lane_of_col == row).astype(BF16)
        wides = [jnp.dot(w2, expand, preferred_element_type=F32) for w2 in split]
        for half in range(2):
            j = 2 * jj + half
            cols = slice(half * LANES, (half + 1) * LANES)
            acc, others = None, None
            for o, wide in zip(outs[:-1], wides):
                term = wide[:, cols] * o(j)
                acc = term if acc is None else acc + term
                others = wide[:, cols] if others is None else others + wide[:, cols]
            acc = acc + (1.0 - others) * outs[-1](j)
            out_ref[:, j * LANES:(j + 1) * LANES] = acc.astype(out_ref.dtype)


def _combine(outs, lses, dils, tm=256):
    n, _, _, c = outs[0].shape
    t = outs[0].shape[1] * outs[0].shape[2]
    tm = _pick(t, tm)
    assert all(tm % (16 * d) == 0 for d in dils)
    blk = lambda width: [pl.BlockSpec((None, d, tm // d, width), lambda b, i: (b, 0, i, 0)) for d in dils]
    scratch = []
    for d in dils:
        if d > 1:
            scratch += [pltpu.VMEM((c // LANES, tm, LANES), F32), pltpu.VMEM((tm, LANES), F32)]
    return pl.pallas_call(
        functools.partial(_combine_kernel, dils=tuple(dils)),
        grid=(n, t // tm),
        in_specs=blk(c) + blk(LANES),
        out_specs=pl.BlockSpec((None, tm, c), lambda b, i: (b, i, 0)),
        out_shape=jax.ShapeDtypeStruct((n, t, c), BF16),
        scratch_shapes=scratch,
        compiler_params=_params("parallel", "parallel"),
        name="combine",
    )(*outs, *lses)


def _stack_heads(q, base, t):
    return jnp.concatenate([q[:, (base + g) * HEAD_DIM:(base + g + 1) * HEAD_DIM] for g in range(GQA)], axis=0)


def _sample_a_kernel(sink_ref, q_ref, kc_ref, kn_ref, vc_ref, vn_ref, o_ref, *, kvh, window):
    for b in range(q_ref.shape[0]):
        _sample_a_one(sink_ref, q_ref.at[b], kc_ref.at[b], kn_ref.at[b], vc_ref.at[b], vn_ref.at[b], o_ref.at[b],
                      kvh=kvh, window=window)


def _sample_a_one(sink_ref, q_ref, kc_ref, kn_ref, vc_ref, vn_ref, o_ref, *, kvh, window):
    t = q_ref.shape[0]
    buf = kc_ref.shape[0]
    q = q_ref[...]
    kc = jnp.concatenate([kc_ref[...], kn_ref[...]], axis=0).astype(BF16)
    vc = jnp.concatenate([vc_ref[...], vn_ref[...]], axis=0).astype(BF16)
    rows = GQA * t
    row = lax.broadcasted_iota(jnp.int32, (rows, buf + t), 0)
    col = lax.broadcasted_iota(jnp.int32, (rows, buf + t), 1)
    dist = buf + _mod(row, t) - col
    valid = (dist >= 0) & (dist < window)
    rowg = _div(lax.broadcasted_iota(jnp.int32, (rows, 1), 0), t)
    scores = [lax.dot_general(_stack_heads(q, h * GQA, t).astype(BF16), kc[:, h * HEAD_DIM:(h + 1) * HEAD_DIM],
                              (((1,), (1,)), ((), ())), preferred_element_type=F32) for h in range(kvh)]
    probs = []
    for h in range(kvh):
        s = jnp.where(valid, scores[h], NEG_INF)
        sink = jnp.zeros((rows, 1), F32)
        for g in range(GQA):
            sink = jnp.where(rowg == g, sink_ref[h * GQA + g], sink)
        m = jnp.maximum(jnp.max(s, axis=-1, keepdims=True), sink)
        e = jnp.exp(s - m)
        den = jnp.sum(e, axis=-1, keepdims=True) + jnp.exp(sink - m)
        probs.append((e.astype(BF16), 1.0 / den))
    for h in range(kvh):
        e, inv = probs[h]
        o = jnp.dot(e, vc[:, h * HEAD_DIM:(h + 1) * HEAD_DIM], preferred_element_type=F32) * inv
        for g in range(GQA):
            col0 = (h * GQA + g) * HEAD_DIM
            o_ref[:, col0:col0 + HEAD_DIM] = o[g * t:(g + 1) * t].astype(o_ref.dtype)


def _sample_a_attn(q, kcache, knew, vcache, vnew, sinks):
    n, t, c = q.shape
    buf, ck = kcache.shape[1:]
    bb = _pick(n, 4)
    row3 = lambda b: (b, 0, 0)
    return pl.pallas_call(
        functools.partial(_sample_a_kernel, kvh=ck // HEAD_DIM, window=A_WINDOW),
        grid=(n // bb,),
        in_specs=[pl.BlockSpec(memory_space=pltpu.SMEM),
                  pl.BlockSpec((bb, t, c), row3),
                  pl.BlockSpec((bb, buf, ck), row3), pl.BlockSpec((bb, t, ck), row3),
                  pl.BlockSpec((bb, buf, ck), row3), pl.BlockSpec((bb, t, ck), row3)],
        out_specs=pl.BlockSpec((bb, t, c), row3),
        out_shape=jax.ShapeDtypeStruct((n, t, c), F32),
        compiler_params=_params("parallel"),
        name="sample_a_attn",
    )(sinks.astype(F32), q, kcache, knew, vcache, vnew)


def _sample_b_kernel(q_ref, kt_ref, kn_ref, vt_ref, vn_ref, o_ref, *, kvh):
    t = q_ref.shape[0]
    buf = kt_ref.shape[1]
    c = o_ref.shape[1]
    rows = GQA * t
    q = q_ref[...]
    kn = kn_ref[...].astype(BF16)
    vn = vn_ref[...].astype(BF16)
    tok_n = _mod(lax.broadcasted_iota(jnp.int32, (rows, t), 0), t)
    dist_n = tok_n - lax.broadcasted_iota(jnp.int32, (rows, t), 1)
    ones_c = jnp.ones((HEAD_DIM, buf), BF16)
    ones_n = jnp.ones((t, HEAD_DIM), BF16)
    nt = (((1,), (1,)), ((), ()))
    starts = [max(buf - window, 0) for window, _ in B_GROUPS]
    pairs = [(h, gi) for h in range(kvh) for gi in range(len(B_GROUPS))]
    hsl = lambda h: slice(h * HEAD_DIM, (h + 1) * HEAD_DIM)
    scores = []
    for h, gi in pairs:
        if gi == 0:
            kt_h = kt_ref[hsl(h), :].astype(BF16)
        q_h = _stack_heads(q, gi * (c // HEAD_DIM) + h * GQA, t).astype(BF16)
        scores.append((jnp.dot(q_h, kt_h[:, starts[gi]:], preferred_element_type=F32),
                       lax.dot_general(q_h, kn[:, hsl(h)], nt, preferred_element_type=F32)))
    probs = []
    for (h, gi), (s_c, s_n) in zip(pairs, scores):
        window, d = B_GROUPS[gi]
        span = buf - starts[gi]
        tok = _mod(lax.broadcasted_iota(jnp.int32, (rows, span), 0), t)
        dist_c = buf + tok - (starts[gi] + lax.broadcasted_iota(jnp.int32, (rows, span), 1))
        s_c = jnp.where((_mod(dist_c, d) == 0) & (dist_c <= window), s_c, NEG_INF)
        s_n = jnp.where((dist_n >= 0) & (_mod(dist_n, d) == 0) & (dist_n <= window), s_n, NEG_INF)
        m = jnp.maximum(jnp.max(s_c, axis=-1, keepdims=True), jnp.max(s_n, axis=-1, keepdims=True))
        probs.append((jnp.exp(s_c - m).astype(BF16), jnp.exp(s_n - m).astype(BF16), m))
    results = []
    for (h, gi), (e_c, e_n, m) in zip(pairs, probs):
        if gi == 0:
            vt_h = jnp.concatenate([vt_ref[hsl(h), :].astype(BF16), ones_c, ones_c], axis=0)
            vn_h = jnp.concatenate([vn[:, hsl(h)], ones_n, ones_n], axis=1)
        res = (lax.dot_general(e_c, vt_h[:, starts[gi]:], nt, preferred_element_type=F32)
               + jnp.dot(e_n, vn_h, preferred_element_type=F32))
        results.append((res, m))
    for h in range(kvh):
        outs, lses = [], []
        for res, m in results[h * len(B_GROUPS):(h + 1) * len(B_GROUPS)]:
            den = res[:, LANES:LANES + HEAD_DIM]
            outs.append(res[:, :HEAD_DIM] * (1.0 / den))
            lses.append(jnp.log(den) + m)
        top = functools.reduce(jnp.maximum, lses)
        ws = [jnp.exp(l - top) for l in lses]
        inv = 1.0 / functools.reduce(jnp.add, ws)
        o = functools.reduce(jnp.add, [o_g * (w * inv) for o_g, w in zip(outs, ws)])
        for g in range(GQA):
            col0 = (h * GQA + g) * HEAD_DIM
            o_ref[:, col0:col0 + HEAD_DIM] = o[g * t:(g + 1) * t].astype(o_ref.dtype)


def _sample_b_attn(q, kt_cache, knew, vt_cache, vnew):
    n, t, gc = q.shape
    c = gc // len(B_GROUPS)
    ck, buf = kt_cache.shape[1:]
    assert all(max(buf - w, 0) % LANES == 0 for w, _ in B_GROUPS)
    row3 = lambda b: (b, 0, 0)
    return pl.pallas_call(
        functools.partial(_sample_b_kernel, kvh=ck // HEAD_DIM),
        grid=(n,),
        in_specs=[pl.BlockSpec((None, t, gc), row3),
                  pl.BlockSpec((None, ck, buf), row3), pl.BlockSpec((None, t, ck), row3),
                  pl.BlockSpec((None, ck, buf), row3), pl.BlockSpec((None, t, ck), row3)],
        out_specs=pl.BlockSpec((None, t, c), row3),
        out_shape=jax.ShapeDtypeStruct((n, t, c), F32),
        compiler_params=_params("parallel"),
        name="sample_b_attn",
    )(q, kt_cache, knew, vt_cache, vnew)


def _rope_tables(pos):
    half = HEAD_DIM // 2
    inv = ROPE_THETA ** (-jnp.arange(half, dtype=F32) / half)
    ang = pos.astype(F32)[:, None] * inv[None, :]
    cos, sin = jnp.cos(ang), jnp.sin(ang)
    reps = LANES // HEAD_DIM
    return jnp.tile(jnp.concatenate([cos, cos], axis=1), (1, reps)), jnp.tile(jnp.concatenate([-sin, sin], axis=1), (1, reps))


def _trunk(x, pos, sample_caches, wts, w16):
    (ln_g, ln_b, w_qkv_a, sinks_a, w_o_a, w_kv_b, w_q_b, w_o_b, w_up, w_down) = wts
    n, t, dm = x.shape
    m = n * t
    depth = ln_g.shape[0]
    n_a = w_qkv_a.shape[0]
    alpha = (2 * depth) ** 0.25
    qa = w_o_a.shape[1]
    kva = (w_qkv_a.shape[2] - qa) // 2
    kvb = w_kv_b.shape[1] // 2
    cb = w_o_b.shape[1]
    ng = len(B_GROUPS)
    scale = HEAD_DIM ** -0.5
    is_prompt = sample_caches is None
    rope = _rope_tables(pos)
    if is_prompt:
        rope_by_d = {d: rope if d == 1 else _rope_tables(pos.reshape(t // d, d).T.reshape(t)) for _, d in B_GROUPS}
    if not is_prompt:
        rope = tuple(jnp.tile(r, (n, 1)) for r in rope)
    qdt = BF16 if is_prompt else F32

    def project(key, xin, w, layer, col0, ncols, out_dtype, side=None, side_key=None, **kw):
        if is_prompt:
            res = _proj(xin, w, out_dtype, layer=layer, col0=col0, n=ncols, side=side, **kw)
            w16[key] = res[1]
            if side is not None:
                w16[side_key] = res[2]
            return res[0]
        return _proj(xin, w16[key], out_dtype, **kw)

    x32 = x.reshape(m, dm)
    x16 = x32
    a_states = []
    b_state = None
    b_kv = None
    for layer in range(depth):
        if layer < n_a:
            q = project(("q_a", layer), x16, w_qkv_a, layer, 0, qa, qdt, rope=rope, rope_cols=qa, scale=scale,
                        side=(w_o_a, layer), side_key=("o_a", layer))
            kv = project(("kv_a", layer), x16, w_qkv_a, layer, qa, 2 * kva, F32, rope=rope, rope_cols=kva)
            k, v = kv[:, :kva], kv[:, kva:]
            if is_prompt:
                k3, v3 = k.reshape(n, t, kva), v.reshape(n, t, kva)
                o = _band_attn(q.reshape(n, t, qa), k3, v3, A_WINDOW - 1, sinks=sinks_a[layer]).reshape(m, qa)
                keep = min(A_WINDOW, t)
                a_states.append((k3[:, -keep:], v3[:, -keep:]))
            else:
                ck = sample_caches[0][layer].reshape(n, -1, kva)
                cv = sample_caches[1][layer].reshape(n, -1, kva)
                k3, v3 = k.reshape(n, t, kva), v.reshape(n, t, kva)
                o = _sample_a_attn(q.reshape(n, t, qa), ck, k3, cv, v3, sinks_a[layer]).reshape(m, qa)
                keep = ck.shape[1]
                a_states.append((jnp.concatenate([ck, k3], axis=1)[:, -keep:],
                                 jnp.concatenate([cv, v3], axis=1)[:, -keep:]))
            x32, x16 = _res_ln(o, w16[("o_a", layer)], x32, ln_g[layer, 0], ln_b[layer, 0], alpha)
        else:
            j = layer - n_a
            if j == 0:
                kv = project("kv_b", x16, w_kv_b, None, 0, 2 * kvb, F32, rope=rope, rope_cols=kvb)
                k3, v3 = kv[:, :kvb].reshape(n, t, kvb), kv[:, kvb:].reshape(n, t, kvb)
                if is_prompt:
                    keep = min(max(w for w, _ in B_GROUPS), t)
                    b_state = (k3[:, -keep:], v3[:, -keep:])
                    dils = [d for _, d in B_GROUPS if d > 1]
                    k_res = dict(zip(dils, _by_residue(k3, dils, F32)))
                    v_res = dict(zip(dils, _by_residue(v3, dils, F32)))
                    b_kv = [(k3, v3) if d == 1 else
                            (k_res[d].reshape(n * d, t // d, kvb), v_res[d].reshape(n * d, t // d, kvb))
                            for _, d in B_GROUPS]
                else:
                    ck = sample_caches[2].reshape(n, -1, kvb)
                    cv = sample_caches[3].reshape(n, -1, kvb)
                    keep = ck.shape[1]
                    b_state = (jnp.concatenate([ck, k3], axis=1)[:, -keep:],
                               jnp.concatenate([cv, v3], axis=1)[:, -keep:])
                    b_kv = (jnp.swapaxes(ck, 1, 2), k3, jnp.swapaxes(cv, 1, 2), v3)
            if is_prompt:
                outs, lses = [], []
                dils = [d for _, d in B_GROUPS if d > 1]
                x_res = dict(zip(dils, _by_residue(x32.reshape(n, t, dm), dils, BF16)))
                for gi, (window, d) in enumerate(B_GROUPS):
                    xg = x16 if d == 1 else x_res[d].reshape(m, dm)
                    side = dict(side=(w_o_b, j), side_key=("o_b", j)) if gi == 0 else {}
                    q = project(("q_b", j, gi), xg, w_q_b, j, gi * cb, cb, BF16,
                                rope=rope_by_d[d], rope_cols=cb, scale=scale, **side)
                    o_g, lse_g = _band_attn(q.reshape(n * d, t // d, cb), b_kv[gi][0], b_kv[gi][1],
                                            window // d, with_lse=True)
                    outs.append(o_g.reshape(n, d, t // d, cb))
                    lses.append(lse_g.reshape(n, d, t // d, LANES))
                o = _combine(outs, lses, [d for _, d in B_GROUPS]).reshape(m, cb)
            else:
                q = jnp.concatenate([project(("q_b", j, gi), x16, None, None, 0, cb, qdt,
                                             rope=rope, rope_cols=cb, scale=scale) for gi in range(ng)], axis=1)
                o = _sample_b_attn(q.reshape(n, t, ng * cb), *b_kv).reshape(m, cb)
            x32, x16 = _res_ln(o, w16[("o_b", j)], x32, ln_g[layer, 0], ln_b[layer, 0], alpha)
        hid = project(("up", layer), x16, w_up, layer, 0, w_up.shape[2], BF16, relu2=True,
                      side=(w_down, layer), side_key=("down", layer))
        x32, x16 = _res_ln(hid, w16[("down", layer)], x32, ln_g[layer, 1], ln_b[layer, 1], alpha)
    return x32.reshape(n, t, dm), a_states, b_state


def kernel(x_prompt, x_sample, cache_a_k, cache_a_v, cache_b_k, cache_b_v, ln_g, ln_b,
           w_qkv_a, sinks_a, w_o_a, w_kv_b, w_q_b, w_o_b, w_up, w_down):
    wts = (ln_g, ln_b, w_qkv_a, sinks_a, w_o_a, w_kv_b, w_q_b, w_o_b, w_up, w_down)
    w16 = {}
    kv_heads_a = cache_a_k.shape[3]
    kv_heads_b = cache_b_k.shape[2]
    pos_prompt = jnp.arange(x_prompt.shape[1], dtype=jnp.int32)
    pos_sample = PAST_LEN + jnp.arange(x_sample.shape[1], dtype=jnp.int32)
    y_prompt, a_prompt, b_prompt = _trunk(x_prompt, pos_prompt, None, wts, w16)
    y_sample, a_sample, b_sample = _trunk(x_sample, pos_sample, (cache_a_k, cache_a_v, cache_b_k, cache_b_v), wts, w16)

    def heads(s, kvh):
        return s.reshape(*s.shape[:-1], kvh, HEAD_DIM)

    return (y_prompt, y_sample,
            jnp.stack([heads(s[0], kv_heads_a) for s in a_prompt], axis=0),
            jnp.stack([heads(s[1], kv_heads_a) for s in a_prompt], axis=0),
            jnp.stack([heads(s[0], kv_heads_a) for s in a_sample], axis=0),
            jnp.stack([heads(s[1], kv_heads_a) for s in a_sample], axis=0),
            heads(b_prompt[0], kv_heads_b), heads(b_prompt[1], kv_heads_b),
            heads(b_sample[0], kv_heads_b), heads(b_sample[1], kv_heads_b))
```

```python
import functools

import jax
import jax.numpy as jnp
from jax import lax
from jax.experimental import pallas as pl
from jax.experimental.pallas import tpu as pltpu

HEAD_DIM = 64
GQA = 8
A_WINDOW = 128
B_GROUPS = ((128, 1), (512, 4), (2048, 16))
PAST_LEN = 16384
ROPE_THETA = 10000.0
LN_EPS = 1e-5
BLOCK = 128
NEG_INF = -1e30
LANES = 128
V7X_VMEM_BYTES = 64 * 1024 * 1024
VMEM_LIMIT = V7X_VMEM_BYTES * 7 // 8

PROJ_TM, PROJ_TN = 1024, 1024
RES_TM, RES_TK = 512, 2048
COMBINE_TM = 256
SAMPLE_A_SEQS = 4
LN_ROWS = 16

BF16 = jnp.bfloat16
F32 = jnp.float32


def _mod(x, n):
    assert n & (n - 1) == 0, "power-of-two modulus only"
    return x & (n - 1)


def _div(x, n):
    assert n & (n - 1) == 0, "power-of-two divisor only"
    return x >> (n.bit_length() - 1)


def _params(*sem):
    return pltpu.CompilerParams(dimension_semantics=sem, vmem_limit_bytes=VMEM_LIMIT)


def _pick(n, pref):
    if n <= pref:
        return n
    t = pref
    while n % t:
        t //= 2
    return t


def _rope_chunk(x, cos, sin_signed):
    lane = lax.broadcasted_iota(jnp.int32, x.shape, 1)
    first = _mod(lane, HEAD_DIM) < (HEAD_DIM // 2)
    partner = jnp.where(first, pltpu.roll(x, LANES - HEAD_DIM // 2, 1), pltpu.roll(x, HEAD_DIM // 2, 1))
    return x * cos + partner * sin_signed


def _proj_kernel(*refs, rope_chunks, scale, relu2, cast_w, has_side, delay_tiles):
    refs = list(refs)
    a_ref, w_ref = refs[:2]
    del refs[:2]
    cos_ref, sin_ref = (refs.pop(0), refs.pop(0)) if rope_chunks else (None, None)
    if has_side:
        side_ref = refs.pop(0)
    o_ref = refs.pop(0)
    if has_side:
        side_out_ref = refs.pop(1 if cast_w else 0)
        side_out_ref[...] = side_ref[...].astype(BF16)
    i = pl.program_id(1)
    if cast_w:
        w16_out_ref, w16_ref = refs[:2]

        @pl.when(i == 0)
        def _():
            w16 = w_ref[...].astype(BF16)
            w16_ref[...] = w16
            w16_out_ref[...] = w16

    def product():
        w = w16_ref[...] if cast_w else w_ref[...]
        return jnp.dot(a_ref[...].astype(BF16), w, preferred_element_type=F32)

    def finish(acc):
        cos = cos_ref[...]
        sin = sin_ref[...]
        for c in range(acc.shape[1] // LANES):
            x = acc[:, c * LANES:(c + 1) * LANES]
            if c < rope_chunks:
                x = _rope_chunk(x, cos, sin)
                if scale != 1.0:
                    x = x * scale
            o_ref[:, c * LANES:(c + 1) * LANES] = x.astype(o_ref.dtype)

    if delay_tiles:
        accs = refs[-2:]

        @pl.when(i == 0)
        def _():
            accs[0][...] = product()

        for parity in range(2):
            @pl.when((i > 0) & (i < delay_tiles) & (i % 2 == parity))
            def _():
                accs[parity][...] = product()
                finish(accs[1 - parity])

        @pl.when(i == delay_tiles)
        def _():
            finish(accs[(delay_tiles - 1) % 2])

        return
    acc = product()
    if relu2:
        acc = jnp.square(jnp.maximum(acc, 0.0))
    if not rope_chunks:
        o_ref[...] = acc.astype(o_ref.dtype)
        return
    finish(acc)


def _proj(a, w, out_dtype, *, layer=None, col0=0, n=None, rope=None, rope_cols=0, scale=1.0, relu2=False,
          side=None, tm=PROJ_TM, tn=PROJ_TN):
    m, k = a.shape
    n = w.shape[-1] if n is None else n
    if m <= tm // 2:
        tn *= 2
    elif a.dtype != BF16:
        tm //= 2
    tm = _pick(m, tm)
    tn = _pick(n, tn)
    assert col0 % tn == 0
    joff = col0 // tn
    cast_w = w.dtype != BF16
    rope_chunks = 0
    if rope is not None:
        cos, sin = rope
        p = cos.shape[0]
        tm = _pick(p, tm)
        pb = p // tm
        assert rope_cols == n or tn == n
        rope_chunks = min(rope_cols, tn) // LANES
    nj, ni = n // tn, m // tm
    delay = rope is not None and rope_chunks * LANES == tn and ni >= 4
    row = (lambda i: jnp.minimum(i, ni - 1)) if delay else (lambda i: i)
    done = (lambda i: jnp.maximum(i - 1, 0)) if delay else (lambda i: i)
    w_mode = dict(pipeline_mode=pl.Buffered(1)) if delay and cast_w else {}
    if w.ndim == 3:
        w_spec = pl.BlockSpec((None, k, tn), lambda j, i: (layer, 0, j + joff), **w_mode)
    else:
        w_spec = pl.BlockSpec((k, tn), lambda j, i: (0, j + joff), **w_mode)
    in_specs = [pl.BlockSpec((tm, k), lambda j, i: (row(i), 0)), w_spec]
    args = [a, w]
    if rope is not None:
        in_specs += [pl.BlockSpec((tm, LANES), lambda j, i: (done(i) % pb, 0))] * 2
        args += [cos, sin]
    out_specs = [pl.BlockSpec((tm, tn), lambda j, i: (done(i), j))]
    out_shape = [jax.ShapeDtypeStruct((m, n), out_dtype)]
    scratch = []
    if cast_w:
        out_specs.append(pl.BlockSpec((k, tn), lambda j, i: (0, j)))
        out_shape.append(jax.ShapeDtypeStruct((k, n), BF16))
        scratch.append(pltpu.VMEM((k, tn), BF16))
    if delay:
        scratch += [pltpu.VMEM((tm, tn), F32), pltpu.VMEM((tm, tn), F32)]
    if side is not None:
        w_side, side_layer = side
        _, rows, cols = w_side.shape
        rs = rows // (nj * ni)
        assert rs * nj * ni == rows and rs % 16 == 0
        in_specs.append(pl.BlockSpec((None, rs, cols), lambda j, i: (side_layer, j * ni + row(i), 0)))
        args.append(w_side)
        out_specs.append(pl.BlockSpec((rs, cols), lambda j, i: (j * ni + row(i), 0)))
        out_shape.append(jax.ShapeDtypeStruct((rows, cols), BF16))
    res = pl.pallas_call(
        functools.partial(_proj_kernel, rope_chunks=rope_chunks, scale=scale, relu2=relu2, cast_w=cast_w,
                          has_side=side is not None, delay_tiles=ni if delay else 0),
        grid=(nj, ni + 1 if delay else ni),
        in_specs=in_specs,
        out_specs=out_specs,
        out_shape=out_shape,
        scratch_shapes=scratch,
        compiler_params=_params("parallel", "arbitrary"),
        name="proj",
    )(*args)
    return tuple(res) if len(res) > 1 else res[0]


def _res_ln_kernel(a_ref, w_ref, x_ref, g_ref, b_ref, o32_ref, o16_ref, acc0_ref, acc1_ref, *, alpha, mt, nk):
    i = pl.program_id(0)
    kk = pl.program_id(1)
    accs = (acc0_ref, acc1_ref)
    rc = acc0_ref.shape[0] // nk

    strips = rc // LN_ROWS
    n = acc0_ref.shape[1]

    def matmul(acc_ref, part=0, parts=1):
        cols = slice(part * n // parts, (part + 1) * n // parts)
        part_product = jnp.dot(a_ref[...].astype(BF16), w_ref[:, cols], preferred_element_type=F32)
        if nk == 1:
            acc_ref[:, cols] = part_product
        else:
            acc_ref[:, cols] += part_product

    def layernorm(acc_ref, part=0, parts=1):
        gain = g_ref[...]
        bias = b_ref[...]
        for s in range(part * strips // parts, (part + 1) * strips // parts):
            if nk == 1:
                rows = slice(s * LN_ROWS, (s + 1) * LN_ROWS)
            else:
                rows = pl.ds(pl.multiple_of(kk * rc + s * LN_ROWS, LN_ROWS), LN_ROWS)
            v = alpha * x_ref[rows, :] + acc_ref[rows, :]
            if nk > 1:
                acc_ref[rows, :] = jnp.zeros((LN_ROWS, acc_ref.shape[1]), F32)
            mu = jnp.mean(v, axis=-1, keepdims=True)
            c = v - mu
            var = jnp.mean(c * c, axis=-1, keepdims=True)
            y = c * lax.rsqrt(var + LN_EPS) * gain + bias
            o32_ref[rows, :] = y
            o16_ref[rows, :] = y.astype(BF16)

    @pl.when((i == 0) & (kk == 0))
    def _():
        acc0_ref[...] = jnp.zeros_like(acc0_ref)
        acc1_ref[...] = jnp.zeros_like(acc1_ref)

    @pl.when(i == 0)
    def _():
        matmul(acc0_ref)

    for parity in range(2):
        @pl.when((i > 0) & (i < mt) & (i % 2 == parity))
        def _():
            matmul(accs[parity])
            layernorm(accs[1 - parity])

    @pl.when(i == mt)
    def _():
        layernorm(accs[(mt - 1) % 2])


def _res_ln(a, w, x, g, b, alpha, *, tm=RES_TM, tk=RES_TK):
    m, k = a.shape
    n = w.shape[1]
    if m <= tm // 2:
        tk *= 2
    tm = _pick(m, tm)
    tk = _pick(k, tk)
    while tk < k and (tm // (k // tk)) % LN_ROWS:
        tk *= 2
    mt, nk = m // tm, k // tk
    assert (tm // nk) % LN_ROWS == 0 and tm % nk == 0
    a_map = lambda i, kk: (jnp.minimum(i, mt - 1), jnp.where(i < mt, kk, nk - 1))
    w_map = lambda i, kk: (jnp.where(i < mt, kk, nk - 1), 0)
    prev_map = lambda i, kk: (jnp.maximum(i - 1, 0), 0)
    return pl.pallas_call(
        functools.partial(_res_ln_kernel, alpha=alpha, mt=mt, nk=nk),
        grid=(mt + 1, nk),
        in_specs=[
            pl.BlockSpec((tm, tk), a_map),
            pl.BlockSpec((tk, n), w_map),
            pl.BlockSpec((tm, n), prev_map),
            pl.BlockSpec((1, n), lambda i, kk: (0, 0)),
            pl.BlockSpec((1, n), lambda i, kk: (0, 0)),
        ],
        out_specs=[pl.BlockSpec((tm, n), prev_map), pl.BlockSpec((tm, n), prev_map)],
        out_shape=[jax.ShapeDtypeStruct((m, n), F32), jax.ShapeDtypeStruct((m, n), BF16)],
        scratch_shapes=[pltpu.VMEM((tm, n), F32), pltpu.VMEM((tm, n), F32)],
        compiler_params=_params("arbitrary", "arbitrary"),
        name="res_ln",
    )(a, w, x, g.reshape(1, n), b.reshape(1, n))


def _band_kernel(*refs, n_back, kvh, has_sink, with_lse, qb):
    refs = list(refs)
    sink_ref = refs.pop(0) if has_sink else None
    q_ref, kp_ref, kc_ref, vp_ref, vc_ref, o_ref = refs[:6]
    lse_ref = refs[6] if with_lse else None
    i = pl.program_id(1)
    k_all = jnp.concatenate([kp_ref[...], kc_ref[...]], axis=0).astype(BF16)
    v_all = jnp.concatenate([vp_ref[...], vc_ref[...]], axis=0).astype(BF16)
    qi = lax.broadcasted_iota(jnp.int32, (BLOCK, 2 * BLOCK), 0)
    ki = lax.broadcasted_iota(jnp.int32, (BLOCK, 2 * BLOCK), 1)
    dist = BLOCK + qi - ki
    in_band = (dist >= 0) & (dist <= n_back)
    lane = lax.broadcasted_iota(jnp.int32, (BLOCK, LANES), 1)
    low = lane < HEAD_DIM
    if has_sink:
        assert n_back < BLOCK
        sink_col = ki == 0
        not_row0 = lax.broadcasted_iota(jnp.int32, (2 * BLOCK, HEAD_DIM), 0) > 0
    zero = jnp.zeros((2 * BLOCK, HEAD_DIM), BF16)
    one = jnp.ones((2 * BLOCK, HEAD_DIM), BF16)
    nt = (((1,), (1,)), ((), ()))
    npair = GQA // 2
    for sb in range(qb):
        rows = slice(sb * BLOCK, (sb + 1) * BLOCK)
        q = q_ref[rows, :]
        kk = k_all[sb * BLOCK:(sb + 2) * BLOCK]
        vv = v_all[sb * BLOCK:(sb + 2) * BLOCK]
        valid = in_band & ((ki >= BLOCK) | (i > 0)) if sb == 0 else in_band
        _band_block(q, kk, vv, valid, sink_ref, o_ref.at[rows], lse_ref.at[rows] if with_lse else None,
                    kvh=kvh, consts=(zero, one, nt, npair, lane, low, sink_col if has_sink else None,
                                     not_row0 if has_sink else None))


def _band_block(q, kk, vv, valid, sink_ref, o_ref, lse_ref, *, kvh, consts):
    zero, one, nt, npair, lane, low, sink_col, not_row0 = consts
    has_sink = sink_ref is not None
    with_lse = lse_ref is not None
    lse_tile = jnp.zeros((BLOCK, LANES), F32)
    scores, maxes, probs, results = {}, {}, {}, {}

    def score(h):
        k_h = kk[:, h * HEAD_DIM:(h + 1) * HEAD_DIM]
        k_sel = (jnp.concatenate([k_h, zero], axis=1), jnp.concatenate([zero, k_h], axis=1))
        q_pairs = jnp.concatenate(
            [q[:, (h * GQA + 2 * j) * HEAD_DIM:(h * GQA + 2 * j + 2) * HEAD_DIM] for j in range(npair)], axis=0)
        scores[h] = [lax.dot_general(q_pairs, k_sel[par], nt, preferred_element_type=F32) for par in range(2)]

    def softmax(h):
        maxes[h], probs[h] = [], []
        for par in range(2):
            es = []
            for j in range(npair):
                fill = NEG_INF
                if has_sink:
                    fill = jnp.where(sink_col, sink_ref[h * GQA + 2 * j + par], NEG_INF)
                s_g = jnp.where(valid, scores[h][par][j * BLOCK:(j + 1) * BLOCK], fill)
                m = jnp.max(s_g, axis=-1, keepdims=True)
                es.append(jnp.exp(s_g - m).astype(BF16))
                maxes[h].append(m)
            probs[h].append(jnp.concatenate(es, axis=0))

    def values(h):
        v_h = vv[:, h * HEAD_DIM:(h + 1) * HEAD_DIM]
        if has_sink:
            v_h = jnp.where(not_row0, v_h, zero)
        v_sel = (jnp.concatenate([v_h, zero, one, zero], axis=1),
                 jnp.concatenate([zero, v_h, zero, one], axis=1))
        results[h] = (jnp.dot(probs[h][0], v_sel[0], preferred_element_type=F32)
                      + jnp.dot(probs[h][1], v_sel[1], preferred_element_type=F32))

    for step in range(kvh + 2):
        if step < kvh:
            score(step)
        if 0 <= step - 1 < kvh:
            softmax(step - 1)
        if 0 <= step - 2 < kvh:
            values(step - 2)
    for h in range(kvh):
        for j in range(npair):
            blk = results[h][j * BLOCK:(j + 1) * BLOCK]
            den = blk[:, LANES:]
            col = (h * GQA + 2 * j) * HEAD_DIM
            o_ref[:, col:col + LANES] = (blk[:, :LANES] * (1.0 / den)).astype(o_ref.dtype)
            if with_lse:
                m_pair = jnp.where(low, maxes[h][j], maxes[h][npair + j])
                pair_id = h * npair + j
                lse_tile = jnp.where(_mod(lane, HEAD_DIM) == pair_id, jnp.log(den) + m_pair, lse_tile)
    if with_lse:
        lse_ref[...] = lse_tile


def _band_attn(q, kv, n_back, sinks=None, with_lse=False):
    n, length, c = q.shape
    ck = kv.shape[2] // 2
    kvh = ck // HEAD_DIM
    assert c == kvh * GQA * HEAD_DIM and length % BLOCK == 0 and n_back <= BLOCK
    qb = 2 if length % (2 * BLOCK) == 0 else 1
    nb = length // (qb * BLOCK)
    cur = lambda b, i: (b, i, 0)
    cur_at = lambda col: (lambda b, i: (b, i, col))
    prev_at = lambda col: (lambda b, i: (b, jnp.maximum(qb * i - 1, 0), col))
    in_specs = [
        pl.BlockSpec((None, qb * BLOCK, c), cur),
        pl.BlockSpec((None, BLOCK, ck), prev_at(0)), pl.BlockSpec((None, qb * BLOCK, ck), cur_at(0)),
        pl.BlockSpec((None, BLOCK, ck), prev_at(1)), pl.BlockSpec((None, qb * BLOCK, ck), cur_at(1)),
    ]
    args = [q, kv, kv, kv, kv]
    if sinks is not None:
        in_specs = [pl.BlockSpec(memory_space=pltpu.SMEM)] + in_specs
        args = [sinks.astype(F32)] + args
    out_specs = [pl.BlockSpec((None, qb * BLOCK, c), cur)]
    out_shape = [jax.ShapeDtypeStruct(q.shape, BF16)]
    if with_lse:
        out_specs.append(pl.BlockSpec((None, qb * BLOCK, LANES), cur))
        out_shape.append(jax.ShapeDtypeStruct((n, length, LANES), F32))
    res = pl.pallas_call(
        functools.partial(_band_kernel, n_back=n_back, kvh=kvh, has_sink=sinks is not None, with_lse=with_lse,
                          qb=qb),
        grid=(n, nb),
        in_specs=in_specs,
        out_specs=out_specs,
        out_shape=out_shape,
        compiler_params=_params("parallel", "arbitrary"),
        name="band_attn",
    )(*args)
    return res if with_lse else res[0]


def _permute_kernel(x_ref, *o_refs):
    for o_ref in o_refs:
        d = o_ref.shape[0]
        rows = o_ref.shape[1]
        for r in range(d):
            o_ref[r] = x_ref[pl.ds(r, rows, stride=d), :].astype(o_ref.dtype)


def _by_residue(x, dils, out_dtype):
    n, t, c = x.shape
    return pl.pallas_call(
        _permute_kernel,
        grid=(n, c // LANES),
        in_specs=[pl.BlockSpec((None, t, LANES), lambda b, j: (b, 0, j))],
        out_specs=[pl.BlockSpec((None, d, t // d, LANES), lambda b, j: (b, 0, 0, j)) for d in dils],
        out_shape=[jax.ShapeDtypeStruct((n, d, t // d, c), out_dtype) for d in dils],
        compiler_params=_params("parallel", "parallel"),
        name="by_residue",
    )(x)


def _combine_kernel(*refs, dils):
    ng = len(dils)
    o_refs, l_refs, out_ref = refs[:ng], refs[ng:2 * ng], refs[2 * ng]
    stage = list(refs[2 * ng + 1:])
    c = out_ref.shape[1]
    nchunk = c // LANES
    outs, lses = [], []
    for o_ref, l_ref, d in zip(o_refs, l_refs, dils):
        if d == 1:
            outs.append(lambda j, o_ref=o_ref: o_ref[0, :, j * LANES:(j + 1) * LANES].astype(F32))
            lses.append(l_ref[0])
            continue
        o_st, l_st = stage.pop(0), stage.pop(0)
        rows = o_ref.shape[1]
        for r in range(d):
            l_st[pl.ds(r, rows, stride=d), :] = l_ref[r]
            for j in range(nchunk):
                o_st[j, pl.ds(r, rows, stride=d), :] = o_ref[r, :, j * LANES:(j + 1) * LANES].astype(F32)
        outs.append(lambda j, o_st=o_st: o_st[j])
        lses.append(l_st[...])
    top = functools.reduce(jnp.maximum, lses)
    ws = [jnp.exp(l - top) for l in lses]
    inv = 1.0 / functools.reduce(jnp.add, ws)
    split = []
    for w in ws[:-1]:
        w = w * inv
        hi = w.astype(BF16)
        split.append(jnp.concatenate([hi, (w - hi.astype(F32)).astype(BF16)], axis=1))
    assert nchunk % 2 == 0
    for jj in range(nchunk // 2):
        head_of_col = _div(lax.broadcasted_iota(jnp.int32, (2 * LANES, 2 * LANES), 1) + jj * 2 * LANES, HEAD_DIM)
        lane_of_col = _mod(head_of_col, 2) * HEAD_DIM + _div(head_of_col, 2)
        row = _mod(lax.broadcasted_iota(jnp.int32, (2 * LANES, 2 * LANES), 0), LANES)
        expand = (lane_of_col == row).astype(BF16)
        wides = [jnp.dot(w2, expand, preferred_element_type=F32) for w2 in split]
        for half in range(2):
            j = 2 * jj + half
            cols = slice(half * LANES, (half + 1) * LANES)
            acc, others = None, None
            for o, wide in zip(outs[:-1], wides):
                term = wide[:, cols] * o(j)
                acc = term if acc is None else acc + term
                others = wide[:, cols] if others is None else others + wide[:, cols]
            acc = acc + (1.0 - others) * outs[-1](j)
            out_ref[:, j * LANES:(j + 1) * LANES] = acc.astype(out_ref.dtype)


def _combine(outs, lses, dils, tm=COMBINE_TM):
    n, _, _, c = outs[0].shape
    t = outs[0].shape[1] * outs[0].shape[2]
    tm = _pick(t, tm)
    assert all(tm % (16 * d) == 0 for d in dils)
    blk = lambda width: [pl.BlockSpec((None, d, tm // d, width), lambda b, i: (b, 0, i, 0)) for d in dils]
    scratch = []
    for d in dils:
        if d > 1:
            scratch += [pltpu.VMEM((c // LANES, tm, LANES), F32), pltpu.VMEM((tm, LANES), F32)]
    return pl.pallas_call(
        functools.partial(_combine_kernel, dils=tuple(dils)),
        grid=(n, t // tm),
        in_specs=blk(c) + blk(LANES),
        out_specs=pl.BlockSpec((None, tm, c), lambda b, i: (b, i, 0)),
        out_shape=jax.ShapeDtypeStruct((n, t, c), BF16),
        scratch_shapes=scratch,
        compiler_params=_params("parallel", "parallel"),
        name="combine",
    )(*outs, *lses)


def _stack_heads(q, base, t):
    return jnp.concatenate([q[:, (base + g) * HEAD_DIM:(base + g + 1) * HEAD_DIM] for g in range(GQA)], axis=0)


def _sample_a_kernel(sink_ref, q_ref, kc_ref, kn_ref, vc_ref, vn_ref, o_ref, *, kvh, window):
    for b in range(q_ref.shape[0]):
        _sample_a_one(sink_ref, q_ref.at[b], kc_ref.at[b], kn_ref.at[b], vc_ref.at[b], vn_ref.at[b], o_ref.at[b],
                      kvh=kvh, window=window)


def _sample_a_one(sink_ref, q_ref, kc_ref, kn_ref, vc_ref, vn_ref, o_ref, *, kvh, window):
    t = q_ref.shape[0]
    buf = kc_ref.shape[0]
    q = q_ref[...]
    kc = jnp.concatenate([kc_ref[...], kn_ref[...]], axis=0).astype(BF16)
    vc = jnp.concatenate([vc_ref[...], vn_ref[...]], axis=0).astype(BF16)
    rows = GQA * t
    row = lax.broadcasted_iota(jnp.int32, (rows, buf + t), 0)
    col = lax.broadcasted_iota(jnp.int32, (rows, buf + t), 1)
    dist = buf + _mod(row, t) - col
    valid = (dist >= 0) & (dist < window)
    rowg = _div(lax.broadcasted_iota(jnp.int32, (rows, 1), 0), t)
    scores = [lax.dot_general(_stack_heads(q, h * GQA, t).astype(BF16), kc[:, h * HEAD_DIM:(h + 1) * HEAD_DIM],
                              (((1,), (1,)), ((), ())), preferred_element_type=F32) for h in range(kvh)]
    probs = []
    for h in range(kvh):
        s = jnp.where(valid, scores[h], NEG_INF)
        sink = jnp.zeros((rows, 1), F32)
        for g in range(GQA):
            sink = jnp.where(rowg == g, sink_ref[h * GQA + g], sink)
        m = jnp.maximum(jnp.max(s, axis=-1, keepdims=True), sink)
        e = jnp.exp(s - m)
        den = jnp.sum(e, axis=-1, keepdims=True) + jnp.exp(sink - m)
        probs.append((e.astype(BF16), 1.0 / den))
    for h in range(kvh):
        e, inv = probs[h]
        o = jnp.dot(e, vc[:, h * HEAD_DIM:(h + 1) * HEAD_DIM], preferred_element_type=F32) * inv
        for g in range(GQA):
            col0 = (h * GQA + g) * HEAD_DIM
            o_ref[:, col0:col0 + HEAD_DIM] = o[g * t:(g + 1) * t].astype(o_ref.dtype)


def _sample_a_attn(q, kcache, knew, vcache, vnew, sinks):
    n, t, c = q.shape
    buf, ck = kcache.shape[1:]
    bb = _pick(n, SAMPLE_A_SEQS)
    row3 = lambda b: (b, 0, 0)
    return pl.pallas_call(
        functools.partial(_sample_a_kernel, kvh=ck // HEAD_DIM, window=A_WINDOW),
        grid=(n // bb,),
        in_specs=[pl.BlockSpec(memory_space=pltpu.SMEM),
                  pl.BlockSpec((bb, t, c), row3),
                  pl.BlockSpec((bb, buf, ck), row3), pl.BlockSpec((bb, t, ck), row3),
                  pl.BlockSpec((bb, buf, ck), row3), pl.BlockSpec((bb, t, ck), row3)],
        out_specs=pl.BlockSpec((bb, t, c), row3),
        out_shape=jax.ShapeDtypeStruct((n, t, c), F32),
        compiler_params=_params("parallel"),
        name="sample_a_attn",
    )(sinks.astype(F32), q, kcache, knew, vcache, vnew)


def _sample_b_kernel(q_ref, kt_ref, kn_ref, vt_ref, vn_ref, o_ref, *, kvh):
    t = q_ref.shape[0]
    buf = kt_ref.shape[1]
    c = o_ref.shape[1]
    rows = GQA * t
    q = q_ref[...]
    kn = kn_ref[...].astype(BF16)
    vn = vn_ref[...].astype(BF16)
    tok_n = _mod(lax.broadcasted_iota(jnp.int32, (rows, t), 0), t)
    dist_n = tok_n - lax.broadcasted_iota(jnp.int32, (rows, t), 1)
    ones_c = jnp.ones((HEAD_DIM, buf), BF16)
    ones_n = jnp.ones((t, HEAD_DIM), BF16)
    nt = (((1,), (1,)), ((), ()))
    starts = [max(buf - window, 0) for window, _ in B_GROUPS]
    pairs = [(h, gi) for h in range(kvh) for gi in range(len(B_GROUPS))]
    hsl = lambda h: slice(h * HEAD_DIM, (h + 1) * HEAD_DIM)
    scores = []
    for h, gi in pairs:
        if gi == 0:
            kt_h = kt_ref[hsl(h), :].astype(BF16)
        q_h = _stack_heads(q, gi * (c // HEAD_DIM) + h * GQA, t).astype(BF16)
        scores.append((jnp.dot(q_h, kt_h[:, starts[gi]:], preferred_element_type=F32),
                       lax.dot_general(q_h, kn[:, hsl(h)], nt, preferred_element_type=F32)))
    masks = []
    for (window, d), start in zip(B_GROUPS, starts):
        span = buf - start
        tok = _mod(lax.broadcasted_iota(jnp.int32, (rows, span), 0), t)
        dist_c = buf + tok - (start + lax.broadcasted_iota(jnp.int32, (rows, span), 1))
        masks.append(((_mod(dist_c, d) == 0) & (dist_c <= window),
                      (dist_n >= 0) & (_mod(dist_n, d) == 0) & (dist_n <= window)))
    probs = []
    for (h, gi), (s_c, s_n) in zip(pairs, scores):
        s_c = jnp.where(masks[gi][0], s_c, NEG_INF)
        s_n = jnp.where(masks[gi][1], s_n, NEG_INF)
        m = jnp.maximum(jnp.max(s_c, axis=-1, keepdims=True), jnp.max(s_n, axis=-1, keepdims=True))
        probs.append((jnp.exp(s_c - m).astype(BF16), jnp.exp(s_n - m).astype(BF16), m))
    results = []
    for (h, gi), (e_c, e_n, m) in zip(pairs, probs):
        if gi == 0:
            vt_h = jnp.concatenate([vt_ref[hsl(h), :].astype(BF16), ones_c, ones_c], axis=0)
            vn_h = jnp.concatenate([vn[:, hsl(h)], ones_n, ones_n], axis=1)
        res = (lax.dot_general(e_c, vt_h[:, starts[gi]:], nt, preferred_element_type=F32)
               + jnp.dot(e_n, vn_h, preferred_element_type=F32))
        results.append((res, m))
    for h in range(kvh):
        outs, lses = [], []
        for res, m in results[h * len(B_GROUPS):(h + 1) * len(B_GROUPS)]:
            den = res[:, LANES:LANES + HEAD_DIM]
            outs.append(res[:, :HEAD_DIM] * (1.0 / den))
            lses.append(jnp.log(den) + m)
        top = functools.reduce(jnp.maximum, lses)
        ws = [jnp.exp(l - top) for l in lses]
        inv = 1.0 / functools.reduce(jnp.add, ws)
        o = functools.reduce(jnp.add, [o_g * (w * inv) for o_g, w in zip(outs, ws)])
        for g in range(GQA):
            col0 = (h * GQA + g) * HEAD_DIM
            o_ref[:, col0:col0 + HEAD_DIM] = o[g * t:(g + 1) * t].astype(o_ref.dtype)


def _sample_b_attn(q, kt_cache, knew, vt_cache, vnew):
    n, t, gc = q.shape
    c = gc // len(B_GROUPS)
    ck, buf = kt_cache.shape[1:]
    assert all(max(buf - w, 0) % LANES == 0 for w, _ in B_GROUPS)
    row3 = lambda b: (b, 0, 0)
    return pl.pallas_call(
        functools.partial(_sample_b_kernel, kvh=ck // HEAD_DIM),
        grid=(n,),
        in_specs=[pl.BlockSpec((None, t, gc), row3),
                  pl.BlockSpec((None, ck, buf), row3), pl.BlockSpec((None, t, ck), row3),
                  pl.BlockSpec((None, ck, buf), row3), pl.BlockSpec((None, t, ck), row3)],
        out_specs=pl.BlockSpec((None, t, c), row3),
        out_shape=jax.ShapeDtypeStruct((n, t, c), F32),
        compiler_params=_params("parallel"),
        name="sample_b_attn",
    )(q, kt_cache, knew, vt_cache, vnew)


def _rope_tables(pos):
    half = HEAD_DIM // 2
    inv = ROPE_THETA ** (-jnp.arange(half, dtype=F32) / half)
    ang = pos.astype(F32)[:, None] * inv[None, :]
    cos, sin = jnp.cos(ang), jnp.sin(ang)
    reps = LANES // HEAD_DIM
    return jnp.tile(jnp.concatenate([cos, cos], axis=1), (1, reps)), jnp.tile(jnp.concatenate([-sin, sin], axis=1), (1, reps))


def _trunk(x, pos, sample_caches, wts, w16):
    (ln_g, ln_b, w_qkv_a, sinks_a, w_o_a, w_kv_b, w_q_b, w_o_b, w_up, w_down) = wts
    n, t, dm = x.shape
    m = n * t
    depth = ln_g.shape[0]
    n_a = w_qkv_a.shape[0]
    alpha = (2 * depth) ** 0.25
    qa = w_o_a.shape[1]
    kva = (w_qkv_a.shape[2] - qa) // 2
    kvb = w_kv_b.shape[1] // 2
    cb = w_o_b.shape[1]
    ng = len(B_GROUPS)
    scale = HEAD_DIM ** -0.5
    is_prompt = sample_caches is None
    rope = _rope_tables(pos)
    if is_prompt:
        rope_by_d = {d: rope if d == 1 else _rope_tables(pos.reshape(t // d, d).T.reshape(t)) for _, d in B_GROUPS}
    if not is_prompt:
        rope = tuple(jnp.tile(r, (n, 1)) for r in rope)
    qdt = BF16 if is_prompt else F32

    def project(key, xin, w, layer, col0, ncols, out_dtype, side=None, side_key=None, **kw):
        if is_prompt:
            res = _proj(xin, w, out_dtype, layer=layer, col0=col0, n=ncols, side=side, **kw)
            w16[key] = res[1]
            if side is not None:
                w16[side_key] = res[2]
            return res[0]
        return _proj(xin, w16[key], out_dtype, **kw)

    x32 = x.reshape(m, dm)
    x16 = x32
    a_states = []
    b_state = None
    b_kv = None
    for layer in range(depth):
        if layer < n_a:
            q = project(("q_a", layer), x16, w_qkv_a, layer, 0, qa, qdt, rope=rope, rope_cols=qa, scale=scale,
                        side=(w_o_a, layer), side_key=("o_a", layer))
            kv = project(("kv_a", layer), x16, w_qkv_a, layer, qa, 2 * kva, F32, rope=rope, rope_cols=kva)
            kv3 = kv.reshape(n, t, 2 * kva)
            if is_prompt:
                o = _band_attn(q.reshape(n, t, qa), kv3, A_WINDOW - 1, sinks=sinks_a[layer]).reshape(m, qa)
                keep = min(A_WINDOW, t)
                a_states.append((kv3[:, -keep:, :kva], kv3[:, -keep:, kva:]))
            else:
                ck = sample_caches[0][layer].reshape(n, -1, kva)
                cv = sample_caches[1][layer].reshape(n, -1, kva)
                k3, v3 = kv3[:, :, :kva], kv3[:, :, kva:]
                o = _sample_a_attn(q.reshape(n, t, qa), ck, k3, cv, v3, sinks_a[layer]).reshape(m, qa)
                keep = ck.shape[1]
                a_states.append((jnp.concatenate([ck, k3], axis=1)[:, -keep:],
                                 jnp.concatenate([cv, v3], axis=1)[:, -keep:]))
            x32, x16 = _res_ln(o, w16[("o_a", layer)], x32, ln_g[layer, 0], ln_b[layer, 0], alpha)
        else:
            j = layer - n_a
            if j == 0:
                kv = project("kv_b", x16, w_kv_b, None, 0, 2 * kvb, F32, rope=rope, rope_cols=kvb)
                kv3 = kv.reshape(n, t, 2 * kvb)
                if is_prompt:
                    keep = min(max(w for w, _ in B_GROUPS), t)
                    b_state = (kv3[:, -keep:, :kvb], kv3[:, -keep:, kvb:])
                    dils = [d for _, d in B_GROUPS if d > 1]
                    kv_res = dict(zip(dils, _by_residue(kv3, dils, F32)))
                    b_kv = [kv3 if d == 1 else kv_res[d].reshape(n * d, t // d, 2 * kvb) for _, d in B_GROUPS]
                else:
                    k3, v3 = kv3[:, :, :kvb], kv3[:, :, kvb:]
                    ck = sample_caches[2].reshape(n, -1, kvb)
                    cv = sample_caches[3].reshape(n, -1, kvb)
                    keep = ck.shape[1]
                    b_state = (jnp.concatenate([ck, k3], axis=1)[:, -keep:],
                               jnp.concatenate([cv, v3], axis=1)[:, -keep:])
                    b_kv = (jnp.swapaxes(ck, 1, 2), k3, jnp.swapaxes(cv, 1, 2), v3)
            if is_prompt:
                outs, lses = [], []
                dils = [d for _, d in B_GROUPS if d > 1]
                x_res = dict(zip(dils, _by_residue(x32.reshape(n, t, dm), dils, BF16)))
                for gi, (window, d) in enumerate(B_GROUPS):
                    xg = x16 if d == 1 else x_res[d].reshape(m, dm)
                    side = dict(side=(w_o_b, j), side_key=("o_b", j)) if gi == 0 else {}
                    q = project(("q_b", j, gi), xg, w_q_b, j, gi * cb, cb, BF16,
                                rope=rope_by_d[d], rope_cols=cb, scale=scale, **side)
                    o_g, lse_g = _band_attn(q.reshape(n * d, t // d, cb), b_kv[gi], window // d, with_lse=True)
                    outs.append(o_g.reshape(n, d, t // d, cb))
                    lses.append(lse_g.reshape(n, d, t // d, LANES))
                o = _combine(outs, lses, [d for _, d in B_GROUPS]).reshape(m, cb)
            else:
                q = jnp.concatenate([project(("q_b", j, gi), x16, None, None, 0, cb, qdt,
                                             rope=rope, rope_cols=cb, scale=scale) for gi in range(ng)], axis=1)
                o = _sample_b_attn(q.reshape(n, t, ng * cb), *b_kv).reshape(m, cb)
            x32, x16 = _res_ln(o, w16[("o_b", j)], x32, ln_g[layer, 0], ln_b[layer, 0], alpha)
        hid = project(("up", layer), x16, w_up, layer, 0, w_up.shape[2], BF16, relu2=True,
                      side=(w_down, layer), side_key=("down", layer))
        x32, x16 = _res_ln(hid, w16[("down", layer)], x32, ln_g[layer, 1], ln_b[layer, 1], alpha)
    return x32.reshape(n, t, dm), a_states, b_state


def kernel(x_prompt, x_sample, cache_a_k, cache_a_v, cache_b_k, cache_b_v, ln_g, ln_b,
           w_qkv_a, sinks_a, w_o_a, w_kv_b, w_q_b, w_o_b, w_up, w_down):
    wts = (ln_g, ln_b, w_qkv_a, sinks_a, w_o_a, w_kv_b, w_q_b, w_o_b, w_up, w_down)
    w16 = {}
    kv_heads_a = cache_a_k.shape[3]
    kv_heads_b = cache_b_k.shape[2]
    pos_prompt = jnp.arange(x_prompt.shape[1], dtype=jnp.int32)
    pos_sample = PAST_LEN + jnp.arange(x_sample.shape[1], dtype=jnp.int32)
    y_prompt, a_prompt, b_prompt = _trunk(x_prompt, pos_prompt, None, wts, w16)
    y_sample, a_sample, b_sample = _trunk(x_sample, pos_sample, (cache_a_k, cache_a_v, cache_b_k, cache_b_v), wts, w16)

    def heads(s, kvh):
        return s.reshape(*s.shape[:-1], kvh, HEAD_DIM)

    return (y_prompt, y_sample,
            jnp.stack([heads(s[0], kv_heads_a) for s in a_prompt], axis=0),
            jnp.stack([heads(s[1], kv_heads_a) for s in a_prompt], axis=0),
            jnp.stack([heads(s[0], kv_heads_a) for s in a_sample], axis=0),
            jnp.stack([heads(s[1], kv_heads_a) for s in a_sample], axis=0),
            heads(b_prompt[0], kv_heads_b), heads(b_prompt[1], kv_heads_b),
            heads(b_sample[0], kv_heads_b), heads(b_sample[1], kv_heads_b))
```

```python
import functools

import jax
import jax.numpy as jnp
from jax import lax
from jax.experimental import pallas as pl
from jax.experimental.pallas import tpu as pltpu

HEAD_DIM = 64
GQA = 8
A_WINDOW = 128
B_GROUPS = ((128, 1), (512, 4), (2048, 16))
PAST_LEN = 16384
ROPE_THETA = 10000.0
LN_EPS = 1e-5
BLOCK = 128
NEG_INF = -1e30
LANES = 128
V7X_VMEM_BYTES = 64 * 1024 * 1024
VMEM_LIMIT = V7X_VMEM_BYTES * 7 // 8

PROJ_TM, PROJ_TN = 1024, 1024
UP_TN = 2048
RES_TM, RES_TK = 512, 2048
COMBINE_TM = 256
SAMPLE_A_SEQS = 4
LN_ROWS = 16

BF16 = jnp.bfloat16
F32 = jnp.float32


def _mod(x, n):
    assert n & (n - 1) == 0, "power-of-two modulus only"
    return x & (n - 1)


def _div(x, n):
    assert n & (n - 1) == 0, "power-of-two divisor only"
    return x >> (n.bit_length() - 1)


def _params(*sem):
    return pltpu.CompilerParams(dimension_semantics=sem, vmem_limit_bytes=VMEM_LIMIT)


def _pick(n, pref):
    if n <= pref:
        return n
    t = pref
    while n % t:
        t //= 2
    return t


def _rope_chunk(x, cos, sin_signed):
    lane = lax.broadcasted_iota(jnp.int32, x.shape, 1)
    first = _mod(lane, HEAD_DIM) < (HEAD_DIM // 2)
    partner = jnp.where(first, pltpu.roll(x, LANES - HEAD_DIM // 2, 1), pltpu.roll(x, HEAD_DIM // 2, 1))
    return x * cos + partner * sin_signed


def _proj_kernel(*refs, rope_chunks, scale, relu2, cast_w, has_side, delay_tiles):
    refs = list(refs)
    a_ref, w_ref = refs[:2]
    del refs[:2]
    cos_ref, sin_ref = (refs.pop(0), refs.pop(0)) if rope_chunks else (None, None)
    if has_side:
        side_ref = refs.pop(0)
    o_ref = refs.pop(0)
    if has_side:
        side_out_ref = refs.pop(1 if cast_w else 0)
        side_out_ref[...] = side_ref[...].astype(BF16)
    i = pl.program_id(1)
    if cast_w:
        w16_out_ref, w16_ref = refs[:2]

        @pl.when(i == 0)
        def _():
            w16 = w_ref[...].astype(BF16)
            w16_ref[...] = w16
            w16_out_ref[...] = w16

    def product():
        w = w16_ref[...] if cast_w else w_ref[...]
        return jnp.dot(a_ref[...].astype(BF16), w, preferred_element_type=F32)

    def finish(acc):
        cos = cos_ref[...]
        sin = sin_ref[...]
        for c in range(acc.shape[1] // LANES):
            x = acc[:, c * LANES:(c + 1) * LANES]
            if c < rope_chunks:
                x = _rope_chunk(x, cos, sin)
                if scale != 1.0:
                    x = x * scale
            o_ref[:, c * LANES:(c + 1) * LANES] = x.astype(o_ref.dtype)

    if delay_tiles:
        accs = refs[-2:]

        @pl.when(i == 0)
        def _():
            accs[0][...] = product()

        for parity in range(2):
            @pl.when((i > 0) & (i < delay_tiles) & (i % 2 == parity))
            def _():
                accs[parity][...] = product()
                finish(accs[1 - parity])

        @pl.when(i == delay_tiles)
        def _():
            finish(accs[(delay_tiles - 1) % 2])

        return
    acc = product()
    if relu2:
        acc = jnp.square(jnp.maximum(acc, 0.0))
    if not rope_chunks:
        o_ref[...] = acc.astype(o_ref.dtype)
        return
    finish(acc)


def _proj(a, w, out_dtype, *, layer=None, col0=0, n=None, rope=None, rope_cols=0, scale=1.0, relu2=False,
          side=None, tm=PROJ_TM, tn=PROJ_TN):
    m, k = a.shape
    n = w.shape[-1] if n is None else n
    if m <= tm // 2:
        tn = max(tn, 2 * PROJ_TN)
    elif a.dtype != BF16:
        tm //= 2
    tm = _pick(m, tm)
    tn = _pick(n, tn)
    assert col0 % tn == 0
    joff = col0 // tn
    cast_w = w.dtype != BF16
    rope_chunks = 0
    if rope is not None:
        cos, sin = rope
        p = cos.shape[0]
        tm = _pick(p, tm)
        pb = p // tm
        assert rope_cols == n or tn == n
        rope_chunks = min(rope_cols, tn) // LANES
    nj, ni = n // tn, m // tm
    delay = rope is not None and rope_chunks * LANES == tn and ni >= 4
    row = (lambda i: jnp.minimum(i, ni - 1)) if delay else (lambda i: i)
    done = (lambda i: jnp.maximum(i - 1, 0)) if delay else (lambda i: i)
    w_mode = dict(pipeline_mode=pl.Buffered(1)) if delay and cast_w else {}
    if w.ndim == 3:
        w_spec = pl.BlockSpec((None, k, tn), lambda j, i: (layer, 0, j + joff), **w_mode)
    else:
        w_spec = pl.BlockSpec((k, tn), lambda j, i: (0, j + joff), **w_mode)
    in_specs = [pl.BlockSpec((tm, k), lambda j, i: (row(i), 0)), w_spec]
    args = [a, w]
    if rope is not None:
        in_specs += [pl.BlockSpec((tm, LANES), lambda j, i: (done(i) % pb, 0))] * 2
        args += [cos, sin]
    out_specs = [pl.BlockSpec((tm, tn), lambda j, i: (done(i), j))]
    out_shape = [jax.ShapeDtypeStruct((m, n), out_dtype)]
    scratch = []
    if cast_w:
        out_specs.append(pl.BlockSpec((k, tn), lambda j, i: (0, j)))
        out_shape.append(jax.ShapeDtypeStruct((k, n), BF16))
        scratch.append(pltpu.VMEM((k, tn), BF16))
    if delay:
        scratch += [pltpu.VMEM((tm, tn), F32), pltpu.VMEM((tm, tn), F32)]
    if side is not None:
        w_side, side_layer = side
        _, rows, cols = w_side.shape
        rs = rows // (nj * ni)
        assert rs * nj * ni == rows and rs % 16 == 0
        in_specs.append(pl.BlockSpec((None, rs, cols), lambda j, i: (side_layer, j * ni + row(i), 0)))
        args.append(w_side)
        out_specs.append(pl.BlockSpec((rs, cols), lambda j, i: (j * ni + row(i), 0)))
        out_shape.append(jax.ShapeDtypeStruct((rows, cols), BF16))
    res = pl.pallas_call(
        functools.partial(_proj_kernel, rope_chunks=rope_chunks, scale=scale, relu2=relu2, cast_w=cast_w,
                          has_side=side is not None, delay_tiles=ni if delay else 0),
        grid=(nj, ni + 1 if delay else ni),
        in_specs=in_specs,
        out_specs=out_specs,
        out_shape=out_shape,
        scratch_shapes=scratch,
        compiler_params=_params("parallel", "arbitrary"),
        name="proj",
    )(*args)
    return tuple(res) if len(res) > 1 else res[0]


def _res_ln_kernel(a_ref, w_ref, x_ref, g_ref, b_ref, o32_ref, o16_ref, acc0_ref, acc1_ref, *, alpha, mt, nk):
    i = pl.program_id(0)
    kk = pl.program_id(1)
    accs = (acc0_ref, acc1_ref)
    rc = acc0_ref.shape[0] // nk

    strips = rc // LN_ROWS
    n = acc0_ref.shape[1]

    def matmul(acc_ref, part=0, parts=1):
        cols = slice(part * n // parts, (part + 1) * n // parts)
        part_product = jnp.dot(a_ref[...].astype(BF16), w_ref[:, cols], preferred_element_type=F32)
        if nk == 1:
            acc_ref[:, cols] = part_product
        else:
            acc_ref[:, cols] += part_product

    def layernorm(acc_ref, part=0, parts=1):
        gain = g_ref[...]
        bias = b_ref[...]
        for s in range(part * strips // parts, (part + 1) * strips // parts):
            if nk == 1:
                rows = slice(s * LN_ROWS, (s + 1) * LN_ROWS)
            else:
                rows = pl.ds(pl.multiple_of(kk * rc + s * LN_ROWS, LN_ROWS), LN_ROWS)
            v = alpha * x_ref[rows, :] + acc_ref[rows, :]
            if nk > 1:
                acc_ref[rows, :] = jnp.zeros((LN_ROWS, acc_ref.shape[1]), F32)
            mu = jnp.mean(v, axis=-1, keepdims=True)
            c = v - mu
            var = jnp.mean(c * c, axis=-1, keepdims=True)
            y = c * lax.rsqrt(var + LN_EPS) * gain + bias
            o32_ref[rows, :] = y
            o16_ref[rows, :] = y.astype(BF16)

    @pl.when((i == 0) & (kk == 0))
    def _():
        acc0_ref[...] = jnp.zeros_like(acc0_ref)
        acc1_ref[...] = jnp.zeros_like(acc1_ref)

    @pl.when(i == 0)
    def _():
        matmul(acc0_ref)

    for parity in range(2):
        @pl.when((i > 0) & (i < mt) & (i % 2 == parity))
        def _():
            matmul(accs[parity])
            layernorm(accs[1 - parity])

    @pl.when(i == mt)
    def _():
        layernorm(accs[(mt - 1) % 2])


def _res_ln(a, w, x, g, b, alpha, *, tm=RES_TM, tk=RES_TK):
    m, k = a.shape
    n = w.shape[1]
    tm = _pick(m, tm)
    tk = _pick(k, tk)
    while tk < k and (tm // (k // tk)) % LN_ROWS:
        tk *= 2
    mt, nk = m // tm, k // tk
    assert (tm // nk) % LN_ROWS == 0 and tm % nk == 0
    a_map = lambda i, kk: (jnp.minimum(i, mt - 1), jnp.where(i < mt, kk, nk - 1))
    w_map = lambda i, kk: (jnp.where(i < mt, kk, nk - 1), 0)
    prev_map = lambda i, kk: (jnp.maximum(i - 1, 0), 0)
    return pl.pallas_call(
        functools.partial(_res_ln_kernel, alpha=alpha, mt=mt, nk=nk),
        grid=(mt + 1, nk),
        in_specs=[
            pl.BlockSpec((tm, tk), a_map),
            pl.BlockSpec((tk, n), w_map),
            pl.BlockSpec((tm, n), prev_map),
            pl.BlockSpec((1, n), lambda i, kk: (0, 0)),
            pl.BlockSpec((1, n), lambda i, kk: (0, 0)),
        ],
        out_specs=[pl.BlockSpec((tm, n), prev_map), pl.BlockSpec((tm, n), prev_map)],
        out_shape=[jax.ShapeDtypeStruct((m, n), F32), jax.ShapeDtypeStruct((m, n), BF16)],
        scratch_shapes=[pltpu.VMEM((tm, n), F32), pltpu.VMEM((tm, n), F32)],
        compiler_params=_params("arbitrary", "arbitrary"),
        name="res_ln",
    )(a, w, x, g.reshape(1, n), b.reshape(1, n))


def _band_kernel(*refs, n_back, kvh, has_sink, with_lse, qb, has_side):
    refs = list(refs)
    sink_ref = refs.pop(0) if has_sink else None
    if has_side:
        side_out_ref = refs.pop()
        side_ref = refs.pop(5)
        side_out_ref[...] = side_ref[...].astype(BF16)
    q_ref, kp_ref, kc_ref, vp_ref, vc_ref, o_ref = refs[:6]
    lse_ref = refs[6] if with_lse else None
    i = pl.program_id(1)
    k_all = jnp.concatenate([kp_ref[...], kc_ref[...]], axis=0).astype(BF16)
    v_all = jnp.concatenate([vp_ref[...], vc_ref[...]], axis=0).astype(BF16)
    qi = lax.broadcasted_iota(jnp.int32, (BLOCK, 2 * BLOCK), 0)
    ki = lax.broadcasted_iota(jnp.int32, (BLOCK, 2 * BLOCK), 1)
    dist = BLOCK + qi - ki
    in_band = (dist >= 0) & (dist <= n_back)
    lane = lax.broadcasted_iota(jnp.int32, (BLOCK, LANES), 1)
    low = lane < HEAD_DIM
    if has_sink:
        assert n_back < BLOCK
        sink_col = ki == 0
        not_row0 = lax.broadcasted_iota(jnp.int32, (2 * BLOCK, HEAD_DIM), 0) > 0
    zero = jnp.zeros((2 * BLOCK, HEAD_DIM), BF16)
    one = jnp.ones((2 * BLOCK, HEAD_DIM), BF16)
    nt = (((1,), (1,)), ((), ()))
    npair = GQA // 2
    for sb in range(qb):
        rows = slice(sb * BLOCK, (sb + 1) * BLOCK)
        q = q_ref[rows, :]
        kk = k_all[sb * BLOCK:(sb + 2) * BLOCK]
        vv = v_all[sb * BLOCK:(sb + 2) * BLOCK]
        valid = in_band & ((ki >= BLOCK) | (i > 0)) if sb == 0 else in_band
        _band_block(q, kk, vv, valid, sink_ref, o_ref.at[rows], lse_ref.at[rows] if with_lse else None,
                    kvh=kvh, consts=(zero, one, nt, npair, lane, low, sink_col if has_sink else None,
                                     not_row0 if has_sink else None))


def _band_block(q, kk, vv, valid, sink_ref, o_ref, lse_ref, *, kvh, consts):
    zero, one, nt, npair, lane, low, sink_col, not_row0 = consts
    has_sink = sink_ref is not None
    with_lse = lse_ref is not None
    lse_tile = jnp.zeros((BLOCK, LANES), F32)
    scores, maxes, probs, results = {}, {}, {}, {}

    def score(h):
        k_h = kk[:, h * HEAD_DIM:(h + 1) * HEAD_DIM]
        k_sel = (jnp.concatenate([k_h, zero], axis=1), jnp.concatenate([zero, k_h], axis=1))
        q_pairs = jnp.concatenate(
            [q[:, (h * GQA + 2 * j) * HEAD_DIM:(h * GQA + 2 * j + 2) * HEAD_DIM] for j in range(npair)], axis=0)
        scores[h] = [lax.dot_general(q_pairs, k_sel[par], nt, preferred_element_type=F32) for par in range(2)]

    def softmax(h):
        maxes[h], probs[h] = [], []
        for par in range(2):
            es = []
            for j in range(npair):
                fill = NEG_INF
                if has_sink:
                    fill = jnp.where(sink_col, sink_ref[h * GQA + 2 * j + par], NEG_INF)
                s_g = jnp.where(valid, scores[h][par][j * BLOCK:(j + 1) * BLOCK], fill)
                m = jnp.max(s_g, axis=-1, keepdims=True)
                es.append(jnp.exp(s_g - m).astype(BF16))
                maxes[h].append(m)
            probs[h].append(jnp.concatenate(es, axis=0))

    def values(h):
        v_h = vv[:, h * HEAD_DIM:(h + 1) * HEAD_DIM]
        if has_sink:
            v_h = jnp.where(not_row0, v_h, zero)
        v_sel = (jnp.concatenate([v_h, zero, one, zero], axis=1),
                 jnp.concatenate([zero, v_h, zero, one], axis=1))
        results[h] = (jnp.dot(probs[h][0], v_sel[0], preferred_element_type=F32)
                      + jnp.dot(probs[h][1], v_sel[1], preferred_element_type=F32))

    for step in range(kvh + 2):
        if step < kvh:
            score(step)
        if 0 <= step - 1 < kvh:
            softmax(step - 1)
        if 0 <= step - 2 < kvh:
            values(step - 2)
    for h in range(kvh):
        for j in range(npair):
            blk = results[h][j * BLOCK:(j + 1) * BLOCK]
            den = blk[:, LANES:]
            col = (h * GQA + 2 * j) * HEAD_DIM
            o_ref[:, col:col + LANES] = (blk[:, :LANES] * (1.0 / den)).astype(o_ref.dtype)
            if with_lse:
                m_pair = jnp.where(low, maxes[h][j], maxes[h][npair + j])
                pair_id = h * npair + j
                lse_tile = jnp.where(_mod(lane, HEAD_DIM) == pair_id, jnp.log(den) + m_pair, lse_tile)
    if with_lse:
        lse_ref[...] = lse_tile


def _band_attn(q, kv, n_back, sinks=None, with_lse=False, side=None):
    n, length, c = q.shape
    ck = kv.shape[2] // 2
    kvh = ck // HEAD_DIM
    assert c == kvh * GQA * HEAD_DIM and length % BLOCK == 0 and n_back <= BLOCK
    qb = 2 if length % (2 * BLOCK) == 0 else 1
    nb = length // (qb * BLOCK)
    cur = lambda b, i: (b, i, 0)
    cur_at = lambda col: (lambda b, i: (b, i, col))
    prev_at = lambda col: (lambda b, i: (b, jnp.maximum(qb * i - 1, 0), col))
    in_specs = [
        pl.BlockSpec((None, qb * BLOCK, c), cur),
        pl.BlockSpec((None, BLOCK, ck), prev_at(0)), pl.BlockSpec((None, qb * BLOCK, ck), cur_at(0)),
        pl.BlockSpec((None, BLOCK, ck), prev_at(1)), pl.BlockSpec((None, qb * BLOCK, ck), cur_at(1)),
    ]
    args = [q, kv, kv, kv, kv]
    out_specs = [pl.BlockSpec((None, qb * BLOCK, c), cur)]
    out_shape = [jax.ShapeDtypeStruct(q.shape, BF16)]
    if with_lse:
        out_specs.append(pl.BlockSpec((None, qb * BLOCK, LANES), cur))
        out_shape.append(jax.ShapeDtypeStruct((n, length, LANES), F32))
    if side is not None:
        w_side, side_layer = side
        _, rows, cols = w_side.shape
        rs = rows // (n * nb)
        assert rs * n * nb == rows and rs % 16 == 0
        in_specs.append(pl.BlockSpec((None, rs, cols), lambda b, i: (side_layer, b * nb + i, 0)))
        args.append(w_side)
        out_specs.append(pl.BlockSpec((rs, cols), lambda b, i: (b * nb + i, 0)))
        out_shape.append(jax.ShapeDtypeStruct((rows, cols), BF16))
    if sinks is not None:
        in_specs = [pl.BlockSpec(memory_space=pltpu.SMEM)] + in_specs
        args = [sinks.astype(F32)] + args
    res = pl.pallas_call(
        functools.partial(_band_kernel, n_back=n_back, kvh=kvh, has_sink=sinks is not None, with_lse=with_lse,
                          qb=qb, has_side=side is not None),
        grid=(n, nb),
        in_specs=in_specs,
        out_specs=out_specs,
        out_shape=out_shape,
        compiler_params=_params("parallel", "arbitrary"),
        name="band_attn",
    )(*args)
    return res if len(res) > 1 else res[0]


def _permute_kernel(x_ref, *o_refs):
    for o_ref in o_refs:
        d = o_ref.shape[0]
        rows = o_ref.shape[1]
        for r in range(d):
            o_ref[r] = x_ref[pl.ds(r, rows, stride=d), :].astype(o_ref.dtype)


def _by_residue(x, dils, out_dtype):
    n, t, c = x.shape
    return pl.pallas_call(
        _permute_kernel,
        grid=(n, c // LANES),
        in_specs=[pl.BlockSpec((None, t, LANES), lambda b, j: (b, 0, j))],
        out_specs=[pl.BlockSpec((None, d, t // d, LANES), lambda b, j: (b, 0, 0, j)) for d in dils],
        out_shape=[jax.ShapeDtypeStruct((n, d, t // d, c), out_dtype) for d in dils],
        compiler_params=_params("parallel", "parallel"),
        name="by_residue",
    )(x)


def _combine_kernel(*refs, dils):
    ng = len(dils)
    o_refs, l_refs, out_ref = refs[:ng], refs[ng:2 * ng], refs[2 * ng]
    stage = list(refs[2 * ng + 1:])
    c = out_ref.shape[1]
    nchunk = c // LANES
    outs, lses = [], []
    for o_ref, l_ref, d in zip(o_refs, l_refs, dils):
        if d == 1:
            outs.append(lambda j, o_ref=o_ref: o_ref[0, :, j * LANES:(j + 1) * LANES].astype(F32))
            lses.append(l_ref[0])
            continue
        o_st, l_st = stage.pop(0), stage.pop(0)
        rows = o_ref.shape[1]
        for r in range(d):
            l_st[pl.ds(r, rows, stride=d), :] = l_ref[r]
            for j in range(nchunk):
                o_st[j, pl.ds(r, rows, stride=d), :] = o_ref[r, :, j * LANES:(j + 1) * LANES].astype(F32)
        outs.append(lambda j, o_st=o_st: o_st[j])
        lses.append(l_st[...])
    top = functools.reduce(jnp.maximum, lses)
    ws = [jnp.exp(l - top) for l in lses]
    inv = 1.0 / functools.reduce(jnp.add, ws)
    split = []
    for w in ws[:-1]:
        w = w * inv
        hi = w.astype(BF16)
        split.append(jnp.concatenate([hi, (w - hi.astype(F32)).astype(BF16)], axis=1))
    assert nchunk % 2 == 0
    for jj in range(nchunk // 2):
        head_of_col = _div(lax.broadcasted_iota(jnp.int32, (2 * LANES, 2 * LANES), 1) + jj * 2 * LANES, HEAD_DIM)
        lane_of_col = _mod(head_of_col, 2) * HEAD_DIM + _div(head_of_col, 2)
        row = _mod(lax.broadcasted_iota(jnp.int32, (2 * LANES, 2 * LANES), 0), LANES)
        expand = (lane_of_col == row).astype(BF16)
        wides = [jnp.dot(w2, expand, preferred_element_type=F32) for w2 in split]
        for half in range(2):
            j = 2 * jj + half
            cols = slice(half * LANES, (half + 1) * LANES)
            acc, others = None, None
            for o, wide in zip(outs[:-1], wides):
                term = wide[:, cols] * o(j)
                acc = term if acc is None else acc + term
                others = wide[:, cols] if others is None else others + wide[:, cols]
            acc = acc + (1.0 - others) * outs[-1](j)
            out_ref[:, j * LANES:(j + 1) * LANES] = acc.astype(out_ref.dtype)


def _combine(outs, lses, dils, tm=COMBINE_TM):
    n, _, _, c = outs[0].shape
    t = outs[0].shape[1] * outs[0].shape[2]
    tm = _pick(t, tm)
    assert all(tm % (16 * d) == 0 for d in dils)
    blk = lambda width: [pl.BlockSpec((None, d, tm // d, width), lambda b, i: (b, 0, i, 0)) for d in dils]
    scratch = []
    for d in dils:
        if d > 1:
            scratch += [pltpu.VMEM((c // LANES, tm, LANES), F32), pltpu.VMEM((tm, LANES), F32)]
    return pl.pallas_call(
        functools.partial(_combine_kernel, dils=tuple(dils)),
        grid=(n, t // tm),
        in_specs=blk(c) + blk(LANES),
        out_specs=pl.BlockSpec((None, tm, c), lambda b, i: (b, i, 0)),
        out_shape=jax.ShapeDtypeStruct((n, t, c), BF16),
        scratch_shapes=scratch,
        compiler_params=_params("parallel", "parallel"),
        name="combine",
    )(*outs, *lses)


def _stack_heads(q, base, t):
    return jnp.concatenate([q[:, (base + g) * HEAD_DIM:(base + g + 1) * HEAD_DIM] for g in range(GQA)], axis=0)


def _sample_a_kernel(sink_ref, q_ref, kc_ref, kn_ref, vc_ref, vn_ref, o_ref, *, kvh, window):
    for b in range(q_ref.shape[0]):
        _sample_a_one(sink_ref, q_ref.at[b], kc_ref.at[b], kn_ref.at[b], vc_ref.at[b], vn_ref.at[b], o_ref.at[b],
                      kvh=kvh, window=window)


def _sample_a_one(sink_ref, q_ref, kc_ref, kn_ref, vc_ref, vn_ref, o_ref, *, kvh, window):
    t = q_ref.shape[0]
    buf = kc_ref.shape[0]
    q = q_ref[...]
    kc = jnp.concatenate([kc_ref[...], kn_ref[...]], axis=0).astype(BF16)
    vc = jnp.concatenate([vc_ref[...], vn_ref[...]], axis=0).astype(BF16)
    rows = GQA * t
    row = lax.broadcasted_iota(jnp.int32, (rows, buf + t), 0)
    col = lax.broadcasted_iota(jnp.int32, (rows, buf + t), 1)
    dist = buf + _mod(row, t) - col
    valid = (dist >= 0) & (dist < window)
    rowg = _div(lax.broadcasted_iota(jnp.int32, (rows, 1), 0), t)
    scores = [lax.dot_general(_stack_heads(q, h * GQA, t).astype(BF16), kc[:, h * HEAD_DIM:(h + 1) * HEAD_DIM],
                              (((1,), (1,)), ((), ())), preferred_element_type=F32) for h in range(kvh)]
    probs = []
    for h in range(kvh):
        s = jnp.where(valid, scores[h], NEG_INF)
        sink = jnp.zeros((rows, 1), F32)
        for g in range(GQA):
            sink = jnp.where(rowg == g, sink_ref[h * GQA + g], sink)
        m = jnp.maximum(jnp.max(s, axis=-1, keepdims=True), sink)
        e = jnp.exp(s - m)
        den = jnp.sum(e, axis=-1, keepdims=True) + jnp.exp(sink - m)
        probs.append((e.astype(BF16), 1.0 / den))
    for h in range(kvh):
        e, inv = probs[h]
        o = jnp.dot(e, vc[:, h * HEAD_DIM:(h + 1) * HEAD_DIM], preferred_element_type=F32) * inv
        for g in range(GQA):
            col0 = (h * GQA + g) * HEAD_DIM
            o_ref[:, col0:col0 + HEAD_DIM] = o[g * t:(g + 1) * t].astype(o_ref.dtype)


def _sample_a_attn(q, kcache, knew, vcache, vnew, sinks):
    n, t, c = q.shape
    buf, ck = kcache.shape[1:]
    bb = _pick(n, SAMPLE_A_SEQS)
    row3 = lambda b: (b, 0, 0)
    return pl.pallas_call(
        functools.partial(_sample_a_kernel, kvh=ck // HEAD_DIM, window=A_WINDOW),
        grid=(n // bb,),
        in_specs=[pl.BlockSpec(memory_space=pltpu.SMEM),
                  pl.BlockSpec((bb, t, c), row3),
                  pl.BlockSpec((bb, buf, ck), row3), pl.BlockSpec((bb, t, ck), row3),
                  pl.BlockSpec((bb, buf, ck), row3), pl.BlockSpec((bb, t, ck), row3)],
        out_specs=pl.BlockSpec((bb, t, c), row3),
        out_shape=jax.ShapeDtypeStruct((n, t, c), F32),
        compiler_params=_params("parallel"),
        name="sample_a_attn",
    )(sinks.astype(F32), q, kcache, knew, vcache, vnew)


def _sample_b_kernel(q_ref, kt_ref, kn_ref, vt_ref, vn_ref, o_ref, *, kvh):
    t = q_ref.shape[0]
    buf = kt_ref.shape[1]
    c = o_ref.shape[1]
    rows = GQA * t
    q = q_ref[...]
    kn = kn_ref[...].astype(BF16)
    vn = vn_ref[...].astype(BF16)
    tok_n = _mod(lax.broadcasted_iota(jnp.int32, (rows, t), 0), t)
    dist_n = tok_n - lax.broadcasted_iota(jnp.int32, (rows, t), 1)
    ones_c = jnp.ones((HEAD_DIM, buf), BF16)
    ones_n = jnp.ones((t, HEAD_DIM), BF16)
    nt = (((1,), (1,)), ((), ()))
    starts = [max(buf - window, 0) for window, _ in B_GROUPS]
    pairs = [(h, gi) for h in range(kvh) for gi in range(len(B_GROUPS))]
    hsl = lambda h: slice(h * HEAD_DIM, (h + 1) * HEAD_DIM)
    scores = []
    for h, gi in pairs:
        if gi == 0:
            kt_h = kt_ref[hsl(h), :].astype(BF16)
        q_h = _stack_heads(q, gi * (c // HEAD_DIM) + h * GQA, t).astype(BF16)
        scores.append((jnp.dot(q_h, kt_h[:, starts[gi]:], preferred_element_type=F32),
                       lax.dot_general(q_h, kn[:, hsl(h)], nt, preferred_element_type=F32)))
    masks = []
    for (window, d), start in zip(B_GROUPS, starts):
        span = buf - start
        tok = _mod(lax.broadcasted_iota(jnp.int32, (rows, span), 0), t)
        dist_c = buf + tok - (start + lax.broadcasted_iota(jnp.int32, (rows, span), 1))
        masks.append(((_mod(dist_c, d) == 0) & (dist_c <= window),
                      (dist_n >= 0) & (_mod(dist_n, d) == 0) & (dist_n <= window)))
    probs = []
    for (h, gi), (s_c, s_n) in zip(pairs, scores):
        s_c = jnp.where(masks[gi][0], s_c, NEG_INF)
        s_n = jnp.where(masks[gi][1], s_n, NEG_INF)
        m = jnp.maximum(jnp.max(s_c, axis=-1, keepdims=True), jnp.max(s_n, axis=-1, keepdims=True))
        probs.append((jnp.exp(s_c - m).astype(BF16), jnp.exp(s_n - m).astype(BF16), m))
    results = []
    for (h, gi), (e_c, e_n, m) in zip(pairs, probs):
        if gi == 0:
            vt_h = jnp.concatenate([vt_ref[hsl(h), :].astype(BF16), ones_c, ones_c], axis=0)
            vn_h = jnp.concatenate([vn[:, hsl(h)], ones_n, ones_n], axis=1)
        res = (lax.dot_general(e_c, vt_h[:, starts[gi]:], nt, preferred_element_type=F32)
               + jnp.dot(e_n, vn_h, preferred_element_type=F32))
        results.append((res, m))
    for h in range(kvh):
        outs, lses = [], []
        for res, m in results[h * len(B_GROUPS):(h + 1) * len(B_GROUPS)]:
            den = res[:, LANES:LANES + HEAD_DIM]
            outs.append(res[:, :HEAD_DIM] * (1.0 / den))
            lses.append(jnp.log(den) + m)
        top = functools.reduce(jnp.maximum, lses)
        ws = [jnp.exp(l - top) for l in lses]
        inv = 1.0 / functools.reduce(jnp.add, ws)
        o = functools.reduce(jnp.add, [o_g * (w * inv) for o_g, w in zip(outs, ws)])
        for g in range(GQA):
            col0 = (h * GQA + g) * HEAD_DIM
            o_ref[:, col0:col0 + HEAD_DIM] = o[g * t:(g + 1) * t].astype(o_ref.dtype)


def _sample_b_attn(q, kt_cache, knew, vt_cache, vnew):
    n, t, gc = q.shape
    c = gc // len(B_GROUPS)
    ck, buf = kt_cache.shape[1:]
    assert all(max(buf - w, 0) % LANES == 0 for w, _ in B_GROUPS)
    row3 = lambda b: (b, 0, 0)
    return pl.pallas_call(
        functools.partial(_sample_b_kernel, kvh=ck // HEAD_DIM),
        grid=(n,),
        in_specs=[pl.BlockSpec((None, t, gc), row3),
                  pl.BlockSpec((None, ck, buf), row3), pl.BlockSpec((None, t, ck), row3),
                  pl.BlockSpec((None, ck, buf), row3), pl.BlockSpec((None, t, ck), row3)],
        out_specs=pl.BlockSpec((None, t, c), row3),
        out_shape=jax.ShapeDtypeStruct((n, t, c), F32),
        compiler_params=_params("parallel"),
        name="sample_b_attn",
    )(q, kt_cache, knew, vt_cache, vnew)


def _rope_tables(pos):
    half = HEAD_DIM // 2
    inv = ROPE_THETA ** (-jnp.arange(half, dtype=F32) / half)
    ang = pos.astype(F32)[:, None] * inv[None, :]
    cos, sin = jnp.cos(ang), jnp.sin(ang)
    reps = LANES // HEAD_DIM
    return jnp.tile(jnp.concatenate([cos, cos], axis=1), (1, reps)), jnp.tile(jnp.concatenate([-sin, sin], axis=1), (1, reps))


def _trunk(x, pos, sample_caches, wts, w16):
    (ln_g, ln_b, w_qkv_a, sinks_a, w_o_a, w_kv_b, w_q_b, w_o_b, w_up, w_down) = wts
    n, t, dm = x.shape
    m = n * t
    depth = ln_g.shape[0]
    n_a = w_qkv_a.shape[0]
    alpha = (2 * depth) ** 0.25
    qa = w_o_a.shape[1]
    kva = (w_qkv_a.shape[2] - qa) // 2
    kvb = w_kv_b.shape[1] // 2
    cb = w_o_b.shape[1]
    ng = len(B_GROUPS)
    scale = HEAD_DIM ** -0.5
    is_prompt = sample_caches is None
    rope = _rope_tables(pos)
    if is_prompt:
        rope_by_d = {d: rope if d == 1 else _rope_tables(pos.reshape(t // d, d).T.reshape(t)) for _, d in B_GROUPS}
    if not is_prompt:
        rope = tuple(jnp.tile(r, (n, 1)) for r in rope)
    qdt = BF16 if is_prompt else F32

    def project(key, xin, w, layer, col0, ncols, out_dtype, side=None, side_key=None, **kw):
        if not is_prompt:
            return _proj(xin, w16[key], out_dtype, **kw)
        if key in w16:
            res = _proj(xin, w16[key], out_dtype, side=side, **kw)
            res = res if side is not None else (res,)
        else:
            res = _proj(xin, w, out_dtype, layer=layer, col0=col0, n=ncols, side=side, **kw)
            w16[key] = res[1]
        if side is not None:
            w16[side_key] = res[-1]
        return res[0]

    x32 = x.reshape(m, dm)
    x16 = x32
    a_states = []
    b_state = None
    b_kv = None
    for layer in range(depth):
        if layer < n_a:
            q = project(("q_a", layer), x16, w_qkv_a, layer, 0, qa, qdt, rope=rope, rope_cols=qa, scale=scale,
                        side=(w_o_a, layer), side_key=("o_a", layer))
            kv = project(("kv_a", layer), x16, w_qkv_a, layer, qa, 2 * kva, F32, rope=rope, rope_cols=kva)
            kv3 = kv.reshape(n, t, 2 * kva)
            if is_prompt:
                o, w16[("up", layer)] = _band_attn(q.reshape(n, t, qa), kv3, A_WINDOW - 1, sinks=sinks_a[layer],
                                                   side=(w_up, layer))
                o = o.reshape(m, qa)
                keep = min(A_WINDOW, t)
                a_states.append((kv3[:, -keep:, :kva], kv3[:, -keep:, kva:]))
            else:
                ck = sample_caches[0][layer].reshape(n, -1, kva)
                cv = sample_caches[1][layer].reshape(n, -1, kva)
                k3, v3 = kv3[:, :, :kva], kv3[:, :, kva:]
                o = _sample_a_attn(q.reshape(n, t, qa), ck, k3, cv, v3, sinks_a[layer]).reshape(m, qa)
                keep = ck.shape[1]
                a_states.append((jnp.concatenate([ck, k3], axis=1)[:, -keep:],
                                 jnp.concatenate([cv, v3], axis=1)[:, -keep:]))
            x32, x16 = _res_ln(o, w16[("o_a", layer)], x32, ln_g[layer, 0], ln_b[layer, 0], alpha)
        else:
            j = layer - n_a
            if j == 0:
                kv = project("kv_b", x16, w_kv_b, None, 0, 2 * kvb, F32, rope=rope, rope_cols=kvb)
                kv3 = kv.reshape(n, t, 2 * kvb)
                if is_prompt:
                    keep = min(max(w for w, _ in B_GROUPS), t)
                    b_state = (kv3[:, -keep:, :kvb], kv3[:, -keep:, kvb:])
                    dils = [d for _, d in B_GROUPS if d > 1]
                    kv_res = dict(zip(dils, _by_residue(kv3, dils, F32)))
                    b_kv = [kv3 if d == 1 else kv_res[d].reshape(n * d, t // d, 2 * kvb) for _, d in B_GROUPS]
                else:
                    k3, v3 = kv3[:, :, :kvb], kv3[:, :, kvb:]
                    ck = sample_caches[2].reshape(n, -1, kvb)
                    cv = sample_caches[3].reshape(n, -1, kvb)
                    keep = ck.shape[1]
                    b_state = (jnp.concatenate([ck, k3], axis=1)[:, -keep:],
                               jnp.concatenate([cv, v3], axis=1)[:, -keep:])
                    b_kv = (jnp.swapaxes(ck, 1, 2), k3, jnp.swapaxes(cv, 1, 2), v3)
            if is_prompt:
                outs, lses = [], []
                dils = [d for _, d in B_GROUPS if d > 1]
                x_res = dict(zip(dils, _by_residue(x32.reshape(n, t, dm), dils, BF16)))
                for gi, (window, d) in enumerate(B_GROUPS):
                    xg = x16 if d == 1 else x_res[d].reshape(m, dm)
                    side = dict(side=(w_o_b, j), side_key=("o_b", j)) if gi == 0 else {}
                    q = project(("q_b", j, gi), xg, w_q_b, j, gi * cb, cb, BF16,
                                rope=rope_by_d[d], rope_cols=cb, scale=scale, **side)
                    res = _band_attn(q.reshape(n * d, t // d, cb), b_kv[gi], window // d, with_lse=True,
                                     side=(w_up, layer) if gi == 0 else None)
                    o_g, lse_g = res[:2]
                    if gi == 0:
                        w16[("up", layer)] = res[2]
                    outs.append(o_g.reshape(n, d, t // d, cb))
                    lses.append(lse_g.reshape(n, d, t // d, LANES))
                o = _combine(outs, lses, [d for _, d in B_GROUPS]).reshape(m, cb)
            else:
                q = jnp.concatenate([project(("q_b", j, gi), x16, None, None, 0, cb, qdt,
                                             rope=rope, rope_cols=cb, scale=scale) for gi in range(ng)], axis=1)
                o = _sample_b_attn(q.reshape(n, t, ng * cb), *b_kv).reshape(m, cb)
            x32, x16 = _res_ln(o, w16[("o_b", j)], x32, ln_g[layer, 0], ln_b[layer, 0], alpha)
        hid = project(("up", layer), x16, w_up, layer, 0, w_up.shape[2], BF16, relu2=True,
                      side=(w_down, layer), side_key=("down", layer), tn=UP_TN)
        x32, x16 = _res_ln(hid, w16[("down", layer)], x32, ln_g[layer, 1], ln_b[layer, 1], alpha)
    return x32.reshape(n, t, dm), a_states, b_state


def kernel(x_prompt, x_sample, cache_a_k, cache_a_v, cache_b_k, cache_b_v, ln_g, ln_b,
           w_qkv_a, sinks_a, w_o_a, w_kv_b, w_q_b, w_o_b, w_up, w_down):
    wts = (ln_g, ln_b, w_qkv_a, sinks_a, w_o_a, w_kv_b, w_q_b, w_o_b, w_up, w_down)
    w16 = {}
    kv_heads_a = cache_a_k.shape[3]
    kv_heads_b = cache_b_k.shape[2]
    pos_prompt = jnp.arange(x_prompt.shape[1], dtype=jnp.int32)
    pos_sample = PAST_LEN + jnp.arange(x_sample.shape[1], dtype=jnp.int32)
    y_prompt, a_prompt, b_prompt = _trunk(x_prompt, pos_prompt, None, wts, w16)
    y_sample, a_sample, b_sample = _trunk(x_sample, pos_sample, (cache_a_k, cache_a_v, cache_b_k, cache_b_v), wts, w16)

    def heads(s, kvh):
        return s.reshape(*s.shape[:-1], kvh, HEAD_DIM)

    return (y_prompt, y_sample,
            jnp.stack([heads(s[0], kv_heads_a) for s in a_prompt], axis=0),
            jnp.stack([heads(s[1], kv_heads_a) for s in a_prompt], axis=0),
            jnp.stack([heads(s[0], kv_heads_a) for s in a_sample], axis=0),
            jnp.stack([heads(s[1], kv_heads_a) for s in a_sample], axis=0),
            heads(b_prompt[0], kv_heads_b), heads(b_prompt[1], kv_heads_b),
            heads(b_sample[0], kv_heads_b), heads(b_sample[1], kv_heads_b))
```

```python
import functools

import jax
import jax.numpy as jnp
from jax import lax
from jax.experimental import pallas as pl
from jax.experimental.pallas import tpu as pltpu

HEAD_DIM = 64
GQA = 8
A_WINDOW = 128
B_GROUPS = ((128, 1), (512, 4), (2048, 16))
PAST_LEN = 16384
ROPE_THETA = 10000.0
LN_EPS = 1e-5
BLOCK = 128
NEG_INF = -1e30
LANES = 128
V7X_VMEM_BYTES = 64 * 1024 * 1024
VMEM_LIMIT = V7X_VMEM_BYTES * 7 // 8

PROJ_TM, PROJ_TN = 1024, 1024
UP_TN = 2048
RES_TM, RES_TK = 512, 2048
COMBINE_TM = 256
SAMPLE_A_SEQS = 4
LN_ROWS = 16

BF16 = jnp.bfloat16
F32 = jnp.float32


def _mod(x, n):
    assert n & (n - 1) == 0, "power-of-two modulus only"
    return x & (n - 1)


def _div(x, n):
    assert n & (n - 1) == 0, "power-of-two divisor only"
    return x >> (n.bit_length() - 1)


def _params(*sem):
    return pltpu.CompilerParams(dimension_semantics=sem, vmem_limit_bytes=VMEM_LIMIT)


def _pick(n, pref):
    if n <= pref:
        return n
    t = pref
    while n % t:
        t //= 2
    return t


def _rope_chunk(x, cos, sin_signed):
    lane = lax.broadcasted_iota(jnp.int32, x.shape, 1)
    first = _mod(lane, HEAD_DIM) < (HEAD_DIM // 2)
    partner = jnp.where(first, pltpu.roll(x, LANES - HEAD_DIM // 2, 1), pltpu.roll(x, HEAD_DIM // 2, 1))
    return x * cos + partner * sin_signed


def _proj_kernel(*refs, rope_chunks, scale, relu2, cast_w, has_side, delay_tiles):
    refs = list(refs)
    a_ref, w_ref = refs[:2]
    del refs[:2]
    cos_ref, sin_ref = (refs.pop(0), refs.pop(0)) if rope_chunks else (None, None)
    if has_side:
        side_ref = refs.pop(0)
    o_ref = refs.pop(0)
    if has_side:
        side_out_ref = refs.pop(1 if cast_w else 0)
        side_out_ref[...] = side_ref[...].astype(BF16)
    i = pl.program_id(1)
    if cast_w:
        w16_out_ref, w16_ref = refs[:2]

        @pl.when(i == 0)
        def _():
            w16 = w_ref[...].astype(BF16)
            w16_ref[...] = w16
            w16_out_ref[...] = w16

    def product():
        w = w16_ref[...] if cast_w else w_ref[...]
        return jnp.dot(a_ref[...].astype(BF16), w, preferred_element_type=F32)

    def finish(acc):
        cos = cos_ref[...]
        sin = sin_ref[...]
        for c in range(acc.shape[1] // LANES):
            x = acc[:, c * LANES:(c + 1) * LANES]
            if c < rope_chunks:
                x = _rope_chunk(x, cos, sin)
                if scale != 1.0:
                    x = x * scale
            o_ref[:, c * LANES:(c + 1) * LANES] = x.astype(o_ref.dtype)

    if delay_tiles:
        accs = refs[-2:]

        @pl.when(i == 0)
        def _():
            accs[0][...] = product()

        for parity in range(2):
            @pl.when((i > 0) & (i < delay_tiles) & (i % 2 == parity))
            def _():
                accs[parity][...] = product()
                finish(accs[1 - parity])

        @pl.when(i == delay_tiles)
        def _():
            finish(accs[(delay_tiles - 1) % 2])

        return
    acc = product()
    if relu2:
        acc = jnp.square(jnp.maximum(acc, 0.0))
    if not rope_chunks:
        o_ref[...] = acc.astype(o_ref.dtype)
        return
    finish(acc)


def _proj(a, w, out_dtype, *, layer=None, col0=0, n=None, rope=None, rope_cols=0, scale=1.0, relu2=False,
          side=None, tm=PROJ_TM, tn=PROJ_TN):
    m, k = a.shape
    n = w.shape[-1] if n is None else n
    if m <= tm // 2:
        tn = max(tn, 2 * PROJ_TN)
    elif a.dtype != BF16:
        tm //= 2
    tm = _pick(m, tm)
    tn = _pick(n, tn)
    assert col0 % tn == 0
    joff = col0 // tn
    cast_w = w.dtype != BF16
    rope_chunks = 0
    if rope is not None:
        cos, sin = rope
        p = cos.shape[0]
        tm = _pick(p, tm)
        pb = p // tm
        assert rope_cols == n or tn == n
        rope_chunks = min(rope_cols, tn) // LANES
    nj, ni = n // tn, m // tm
    delay = rope is not None and rope_chunks * LANES == tn and ni >= 4
    row = (lambda i: jnp.minimum(i, ni - 1)) if delay else (lambda i: i)
    done = (lambda i: jnp.maximum(i - 1, 0)) if delay else (lambda i: i)
    w_mode = dict(pipeline_mode=pl.Buffered(1)) if delay and cast_w else {}
    if w.ndim == 3:
        w_spec = pl.BlockSpec((None, k, tn), lambda j, i: (layer, 0, j + joff), **w_mode)
    else:
        w_spec = pl.BlockSpec((k, tn), lambda j, i: (0, j + joff), **w_mode)
    in_specs = [pl.BlockSpec((tm, k), lambda j, i: (row(i), 0)), w_spec]
    args = [a, w]
    if rope is not None:
        in_specs += [pl.BlockSpec((tm, LANES), lambda j, i: (done(i) % pb, 0))] * 2
        args += [cos, sin]
    out_specs = [pl.BlockSpec((tm, tn), lambda j, i: (done(i), j))]
    out_shape = [jax.ShapeDtypeStruct((m, n), out_dtype)]
    scratch = []
    if cast_w:
        out_specs.append(pl.BlockSpec((k, tn), lambda j, i: (0, j)))
        out_shape.append(jax.ShapeDtypeStruct((k, n), BF16))
        scratch.append(pltpu.VMEM((k, tn), BF16))
    if delay:
        scratch += [pltpu.VMEM((tm, tn), F32), pltpu.VMEM((tm, tn), F32)]
    if side is not None:
        w_side, side_layer = side
        _, rows, cols = w_side.shape
        rs = rows // (nj * ni)
        assert rs * nj * ni == rows and rs % 16 == 0
        in_specs.append(pl.BlockSpec((None, rs, cols), lambda j, i: (side_layer, j * ni + row(i), 0)))
        args.append(w_side)
        out_specs.append(pl.BlockSpec((rs, cols), lambda j, i: (j * ni + row(i), 0)))
        out_shape.append(jax.ShapeDtypeStruct((rows, cols), BF16))
    res = pl.pallas_call(
        functools.partial(_proj_kernel, rope_chunks=rope_chunks, scale=scale, relu2=relu2, cast_w=cast_w,
                          has_side=side is not None, delay_tiles=ni if delay else 0),
        grid=(nj, ni + 1 if delay else ni),
        in_specs=in_specs,
        out_specs=out_specs,
        out_shape=out_shape,
        scratch_shapes=scratch,
        compiler_params=_params("parallel", "arbitrary"),
        name="proj",
    )(*args)
    return tuple(res) if len(res) > 1 else res[0]


def _res_ln_kernel(a_ref, w_ref, x_ref, g_ref, b_ref, o32_ref, o16_ref, acc0_ref, acc1_ref, *, alpha, mt, nk):
    i = pl.program_id(0)
    kk = pl.program_id(1)
    accs = (acc0_ref, acc1_ref)
    rc = acc0_ref.shape[0] // nk

    strips = rc // LN_ROWS
    n = acc0_ref.shape[1]

    def matmul(acc_ref, part=0, parts=1):
        cols = slice(part * n // parts, (part + 1) * n // parts)
        part_product = jnp.dot(a_ref[...].astype(BF16), w_ref[:, cols], preferred_element_type=F32)
        if nk == 1:
            acc_ref[:, cols] = part_product
        else:
            acc_ref[:, cols] += part_product

    def layernorm(acc_ref, part=0, parts=1):
        gain = g_ref[...]
        bias = b_ref[...]
        for s in range(part * strips // parts, (part + 1) * strips // parts):
            if nk == 1:
                rows = slice(s * LN_ROWS, (s + 1) * LN_ROWS)
            else:
                rows = pl.ds(pl.multiple_of(kk * rc + s * LN_ROWS, LN_ROWS), LN_ROWS)
            v = alpha * x_ref[rows, :] + acc_ref[rows, :]
            if nk > 1:
                acc_ref[rows, :] = jnp.zeros((LN_ROWS, acc_ref.shape[1]), F32)
            mu = jnp.mean(v, axis=-1, keepdims=True)
            c = v - mu
            var = jnp.mean(c * c, axis=-1, keepdims=True)
            y = c * lax.rsqrt(var + LN_EPS) * gain + bias
            o32_ref[rows, :] = y
            o16_ref[rows, :] = y.astype(BF16)

    @pl.when((i == 0) & (kk == 0))
    def _():
        acc0_ref[...] = jnp.zeros_like(acc0_ref)
        acc1_ref[...] = jnp.zeros_like(acc1_ref)

    @pl.when(i == 0)
    def _():
        matmul(acc0_ref)

    for parity in range(2):
        @pl.when((i > 0) & (i < mt) & (i % 2 == parity))
        def _():
            matmul(accs[parity])
            layernorm(accs[1 - parity])

    @pl.when(i == mt)
    def _():
        layernorm(accs[(mt - 1) % 2])


def _res_ln(a, w, x, g, b, alpha, *, tm=RES_TM, tk=RES_TK):
    m, k = a.shape
    n = w.shape[1]
    tm = _pick(m, tm)
    tk = _pick(k, tk)
    while tk < k and (tm // (k // tk)) % LN_ROWS:
        tk *= 2
    mt, nk = m // tm, k // tk
    assert (tm // nk) % LN_ROWS == 0 and tm % nk == 0
    a_map = lambda i, kk: (jnp.minimum(i, mt - 1), jnp.where(i < mt, kk, nk - 1))
    w_map = lambda i, kk: (jnp.where(i < mt, kk, nk - 1), 0)
    prev_map = lambda i, kk: (jnp.maximum(i - 1, 0), 0)
    return pl.pallas_call(
        functools.partial(_res_ln_kernel, alpha=alpha, mt=mt, nk=nk),
        grid=(mt + 1, nk),
        in_specs=[
            pl.BlockSpec((tm, tk), a_map),
            pl.BlockSpec((tk, n), w_map),
            pl.BlockSpec((tm, n), prev_map),
            pl.BlockSpec((1, n), lambda i, kk: (0, 0)),
            pl.BlockSpec((1, n), lambda i, kk: (0, 0)),
        ],
        out_specs=[pl.BlockSpec((tm, n), prev_map), pl.BlockSpec((tm, n), prev_map)],
        out_shape=[jax.ShapeDtypeStruct((m, n), F32), jax.ShapeDtypeStruct((m, n), BF16)],
        scratch_shapes=[pltpu.VMEM((tm, n), F32), pltpu.VMEM((tm, n), F32)],
        compiler_params=_params("arbitrary", "arbitrary"),
        name="res_ln",
    )(a, w, x, g.reshape(1, n), b.reshape(1, n))


def _band_kernel(*refs, n_back, kvh, has_sink, with_lse, qb, n_sides):
    refs = list(refs)
    sink_ref = refs.pop(0) if has_sink else None
    side_out_refs = [refs.pop() for _ in range(n_sides)][::-1]
    side_refs = [refs.pop(5) for _ in range(n_sides)]
    for side_ref, side_out_ref in zip(side_refs, side_out_refs):
        side_out_ref[...] = side_ref[...].astype(BF16)
    q_ref, kp_ref, kc_ref, vp_ref, vc_ref, o_ref = refs[:6]
    lse_ref = refs[6] if with_lse else None
    i = pl.program_id(1)
    k_all = jnp.concatenate([kp_ref[...], kc_ref[...]], axis=0).astype(BF16)
    v_all = jnp.concatenate([vp_ref[...], vc_ref[...]], axis=0).astype(BF16)
    qi = lax.broadcasted_iota(jnp.int32, (BLOCK, 2 * BLOCK), 0)
    ki = lax.broadcasted_iota(jnp.int32, (BLOCK, 2 * BLOCK), 1)
    dist = BLOCK + qi - ki
    in_band = (dist >= 0) & (dist <= n_back)
    lane = lax.broadcasted_iota(jnp.int32, (BLOCK, LANES), 1)
    low = lane < HEAD_DIM
    if has_sink:
        assert n_back < BLOCK
        sink_col = ki == 0
        not_row0 = lax.broadcasted_iota(jnp.int32, (2 * BLOCK, HEAD_DIM), 0) > 0
    zero = jnp.zeros((2 * BLOCK, HEAD_DIM), BF16)
    one = jnp.ones((2 * BLOCK, HEAD_DIM), BF16)
    nt = (((1,), (1,)), ((), ()))
    npair = GQA // 2
    for sb in range(qb):
        rows = slice(sb * BLOCK, (sb + 1) * BLOCK)
        q = q_ref[rows, :]
        kk = k_all[sb * BLOCK:(sb + 2) * BLOCK]
        vv = v_all[sb * BLOCK:(sb + 2) * BLOCK]
        valid = in_band & ((ki >= BLOCK) | (i > 0)) if sb == 0 else in_band
        _band_block(q, kk, vv, valid, sink_ref, o_ref.at[rows], lse_ref.at[rows] if with_lse else None,
                    kvh=kvh, consts=(zero, one, nt, npair, lane, low, sink_col if has_sink else None,
                                     not_row0 if has_sink else None))


def _band_block(q, kk, vv, valid, sink_ref, o_ref, lse_ref, *, kvh, consts):
    zero, one, nt, npair, lane, low, sink_col, not_row0 = consts
    has_sink = sink_ref is not None
    with_lse = lse_ref is not None
    lse_tile = jnp.zeros((BLOCK, LANES), F32)
    scores, maxes, probs, results = {}, {}, {}, {}

    def score(h):
        k_h = kk[:, h * HEAD_DIM:(h + 1) * HEAD_DIM]
        k_sel = (jnp.concatenate([k_h, zero], axis=1), jnp.concatenate([zero, k_h], axis=1))
        q_pairs = jnp.concatenate(
            [q[:, (h * GQA + 2 * j) * HEAD_DIM:(h * GQA + 2 * j + 2) * HEAD_DIM] for j in range(npair)], axis=0)
        scores[h] = [lax.dot_general(q_pairs, k_sel[par], nt, preferred_element_type=F32) for par in range(2)]

    def softmax(h):
        maxes[h], probs[h] = [], []
        for par in range(2):
            es = []
            for j in range(npair):
                fill = NEG_INF
                if has_sink:
                    fill = jnp.where(sink_col, sink_ref[h * GQA + 2 * j + par], NEG_INF)
                s_g = jnp.where(valid, scores[h][par][j * BLOCK:(j + 1) * BLOCK], fill)
                m = jnp.max(s_g, axis=-1, keepdims=True)
                es.append(jnp.exp(s_g - m).astype(BF16))
                maxes[h].append(m)
            probs[h].append(jnp.concatenate(es, axis=0))

    def values(h):
        v_h = vv[:, h * HEAD_DIM:(h + 1) * HEAD_DIM]
        if has_sink:
            v_h = jnp.where(not_row0, v_h, zero)
        v_sel = (jnp.concatenate([v_h, zero, one, zero], axis=1),
                 jnp.concatenate([zero, v_h, zero, one], axis=1))
        results[h] = (jnp.dot(probs[h][0], v_sel[0], preferred_element_type=F32)
                      + jnp.dot(probs[h][1], v_sel[1], preferred_element_type=F32))

    for step in range(kvh + 2):
        if step < kvh:
            score(step)
        if 0 <= step - 1 < kvh:
            softmax(step - 1)
        if 0 <= step - 2 < kvh:
            values(step - 2)
    for h in range(kvh):
        for j in range(npair):
            blk = results[h][j * BLOCK:(j + 1) * BLOCK]
            den = blk[:, LANES:]
            col = (h * GQA + 2 * j) * HEAD_DIM
            o_ref[:, col:col + LANES] = (blk[:, :LANES] * (1.0 / den)).astype(o_ref.dtype)
            if with_lse:
                m_pair = jnp.where(low, maxes[h][j], maxes[h][npair + j])
                pair_id = h * npair + j
                lse_tile = jnp.where(_mod(lane, HEAD_DIM) == pair_id, jnp.log(den) + m_pair, lse_tile)
    if with_lse:
        lse_ref[...] = lse_tile


def _band_attn(q, kv, n_back, sinks=None, with_lse=False, sides=()):
    n, length, c = q.shape
    ck = kv.shape[2] // 2
    kvh = ck // HEAD_DIM
    assert c == kvh * GQA * HEAD_DIM and length % BLOCK == 0 and n_back <= BLOCK
    qb = 2 if length % (2 * BLOCK) == 0 else 1
    nb = length // (qb * BLOCK)
    cur = lambda b, i: (b, i, 0)
    cur_at = lambda col: (lambda b, i: (b, i, col))
    prev_at = lambda col: (lambda b, i: (b, jnp.maximum(qb * i - 1, 0), col))
    in_specs = [
        pl.BlockSpec((None, qb * BLOCK, c), cur),
        pl.BlockSpec((None, BLOCK, ck), prev_at(0)), pl.BlockSpec((None, qb * BLOCK, ck), cur_at(0)),
        pl.BlockSpec((None, BLOCK, ck), prev_at(1)), pl.BlockSpec((None, qb * BLOCK, ck), cur_at(1)),
    ]
    args = [q, kv, kv, kv, kv]
    out_specs = [pl.BlockSpec((None, qb * BLOCK, c), cur)]
    out_shape = [jax.ShapeDtypeStruct(q.shape, BF16)]
    if with_lse:
        out_specs.append(pl.BlockSpec((None, qb * BLOCK, LANES), cur))
        out_shape.append(jax.ShapeDtypeStruct((n, length, LANES), F32))
    for w_side, side_layer in sides:
        _, rows, cols = w_side.shape
        rs = rows // (n * nb)
        assert rs * n * nb == rows and rs % 16 == 0
        in_specs.append(pl.BlockSpec((None, rs, cols), lambda b, i, side_layer=side_layer: (side_layer, b * nb + i, 0)))
        args.append(w_side)
        out_specs.append(pl.BlockSpec((rs, cols), lambda b, i: (b * nb + i, 0)))
        out_shape.append(jax.ShapeDtypeStruct((rows, cols), BF16))
    if sinks is not None:
        in_specs = [pl.BlockSpec(memory_space=pltpu.SMEM)] + in_specs
        args = [sinks.astype(F32)] + args
    res = pl.pallas_call(
        functools.partial(_band_kernel, n_back=n_back, kvh=kvh, has_sink=sinks is not None, with_lse=with_lse,
                          qb=qb, n_sides=len(sides)),
        grid=(n, nb),
        in_specs=in_specs,
        out_specs=out_specs,
        out_shape=out_shape,
        compiler_params=_params("parallel", "arbitrary"),
        name="band_attn",
    )(*args)
    return res if len(res) > 1 else res[0]


def _permute_kernel(x_ref, *o_refs):
    for o_ref in o_refs:
        d = o_ref.shape[0]
        rows = o_ref.shape[1]
        for r in range(d):
            o_ref[r] = x_ref[pl.ds(r, rows, stride=d), :].astype(o_ref.dtype)


def _by_residue(x, dils, out_dtype):
    n, t, c = x.shape
    return pl.pallas_call(
        _permute_kernel,
        grid=(n, c // LANES),
        in_specs=[pl.BlockSpec((None, t, LANES), lambda b, j: (b, 0, j))],
        out_specs=[pl.BlockSpec((None, d, t // d, LANES), lambda b, j: (b, 0, 0, j)) for d in dils],
        out_shape=[jax.ShapeDtypeStruct((n, d, t // d, c), out_dtype) for d in dils],
        compiler_params=_params("parallel", "parallel"),
        name="by_residue",
    )(x)


def _combine_kernel(*refs, dils):
    ng = len(dils)
    o_refs, l_refs, out_ref = refs[:ng], refs[ng:2 * ng], refs[2 * ng]
    stage = list(refs[2 * ng + 1:])
    c = out_ref.shape[1]
    nchunk = c // LANES
    outs, lses = [], []
    for o_ref, l_ref, d in zip(o_refs, l_refs, dils):
        if d == 1:
            outs.append(lambda j, o_ref=o_ref: o_ref[0, :, j * LANES:(j + 1) * LANES].astype(F32))
            lses.append(l_ref[0])
            continue
        o_st, l_st = stage.pop(0), stage.pop(0)
        rows = o_ref.shape[1]
        for r in range(d):
            l_st[pl.ds(r, rows, stride=d), :] = l_ref[r]
            for j in range(nchunk):
                o_st[j, pl.ds(r, rows, stride=d), :] = o_ref[r, :, j * LANES:(j + 1) * LANES].astype(F32)
        outs.append(lambda j, o_st=o_st: o_st[j])
        lses.append(l_st[...])
    top = functools.reduce(jnp.maximum, lses)
    ws = [jnp.exp(l - top) for l in lses]
    inv = 1.0 / functools.reduce(jnp.add, ws)
    split = []
    for w in ws[:-1]:
        w = w * inv
        hi = w.astype(BF16)
        split.append(jnp.concatenate([hi, (w - hi.astype(F32)).astype(BF16)], axis=1))
    assert nchunk % 2 == 0
    for jj in range(nchunk // 2):
        head_of_col = _div(lax.broadcasted_iota(jnp.int32, (2 * LANES, 2 * LANES), 1) + jj * 2 * LANES, HEAD_DIM)
        lane_of_col = _mod(head_of_col, 2) * HEAD_DIM + _div(head_of_col, 2)
        row = _mod(lax.broadcasted_iota(jnp.int32, (2 * LANES, 2 * LANES), 0), LANES)
        expand = (lane_of_col == row).astype(BF16)
        wides = [jnp.dot(w2, expand, preferred_element_type=F32) for w2 in split]
        for half in range(2):
            j = 2 * jj + half
            cols = slice(half * LANES, (half + 1) * LANES)
            acc, others = None, None
            for o, wide in zip(outs[:-1], wides):
                term = wide[:, cols] * o(j)
                acc = term if acc is None else acc + term
                others = wide[:, cols] if others is None else others + wide[:, cols]
            acc = acc + (1.0 - others) * outs[-1](j)
            out_ref[:, j * LANES:(j + 1) * LANES] = acc.astype(out_ref.dtype)


def _combine(outs, lses, dils, tm=COMBINE_TM):
    n, _, _, c = outs[0].shape
    t = outs[0].shape[1] * outs[0].shape[2]
    tm = _pick(t, tm)
    assert all(tm % (16 * d) == 0 for d in dils)
    blk = lambda width: [pl.BlockSpec((None, d, tm // d, width), lambda b, i: (b, 0, i, 0)) for d in dils]
    scratch = []
    for d in dils:
        if d > 1:
            scratch += [pltpu.VMEM((c // LANES, tm, LANES), F32), pltpu.VMEM((tm, LANES), F32)]
    return pl.pallas_call(
        functools.partial(_combine_kernel, dils=tuple(dils)),
        grid=(n, t // tm),
        in_specs=blk(c) + blk(LANES),
        out_specs=pl.BlockSpec((None, tm, c), lambda b, i: (b, i, 0)),
        out_shape=jax.ShapeDtypeStruct((n, t, c), BF16),
        scratch_shapes=scratch,
        compiler_params=_params("parallel", "parallel"),
        name="combine",
    )(*outs, *lses)


def _stack_heads(q, base, t):
    return jnp.concatenate([q[:, (base + g) * HEAD_DIM:(base + g + 1) * HEAD_DIM] for g in range(GQA)], axis=0)


def _sample_a_kernel(sink_ref, q_ref, kc_ref, kn_ref, vc_ref, vn_ref, o_ref, *, kvh, window):
    for b in range(q_ref.shape[0]):
        _sample_a_one(sink_ref, q_ref.at[b], kc_ref.at[b], kn_ref.at[b], vc_ref.at[b], vn_ref.at[b], o_ref.at[b],
                      kvh=kvh, window=window)


def _sample_a_one(sink_ref, q_ref, kc_ref, kn_ref, vc_ref, vn_ref, o_ref, *, kvh, window):
    t = q_ref.shape[0]
    buf = kc_ref.shape[0]
    q = q_ref[...]
    kc = jnp.concatenate([kc_ref[...], kn_ref[...]], axis=0).astype(BF16)
    vc = jnp.concatenate([vc_ref[...], vn_ref[...]], axis=0).astype(BF16)
    rows = GQA * t
    row = lax.broadcasted_iota(jnp.int32, (rows, buf + t), 0)
    col = lax.broadcasted_iota(jnp.int32, (rows, buf + t), 1)
    dist = buf + _mod(row, t) - col
    valid = (dist >= 0) & (dist < window)
    rowg = _div(lax.broadcasted_iota(jnp.int32, (rows, 1), 0), t)
    scores = [lax.dot_general(_stack_heads(q, h * GQA, t).astype(BF16), kc[:, h * HEAD_DIM:(h + 1) * HEAD_DIM],
                              (((1,), (1,)), ((), ())), preferred_element_type=F32) for h in range(kvh)]
    probs = []
    for h in range(kvh):
        s = jnp.where(valid, scores[h], NEG_INF)
        sink = jnp.zeros((rows, 1), F32)
        for g in range(GQA):
            sink = jnp.where(rowg == g, sink_ref[h * GQA + g], sink)
        m = jnp.maximum(jnp.max(s, axis=-1, keepdims=True), sink)
        e = jnp.exp(s - m)
        den = jnp.sum(e, axis=-1, keepdims=True) + jnp.exp(sink - m)
        probs.append((e.astype(BF16), 1.0 / den))
    for h in range(kvh):
        e, inv = probs[h]
        o = jnp.dot(e, vc[:, h * HEAD_DIM:(h + 1) * HEAD_DIM], preferred_element_type=F32) * inv
        for g in range(GQA):
            col0 = (h * GQA + g) * HEAD_DIM
            o_ref[:, col0:col0 + HEAD_DIM] = o[g * t:(g + 1) * t].astype(o_ref.dtype)


def _sample_a_attn(q, kcache, knew, vcache, vnew, sinks):
    n, t, c = q.shape
    buf, ck = kcache.shape[1:]
    bb = _pick(n, SAMPLE_A_SEQS)
    row3 = lambda b: (b, 0, 0)
    return pl.pallas_call(
        functools.partial(_sample_a_kernel, kvh=ck // HEAD_DIM, window=A_WINDOW),
        grid=(n // bb,),
        in_specs=[pl.BlockSpec(memory_space=pltpu.SMEM),
                  pl.BlockSpec((bb, t, c), row3),
                  pl.BlockSpec((bb, buf, ck), row3), pl.BlockSpec((bb, t, ck), row3),
                  pl.BlockSpec((bb, buf, ck), row3), pl.BlockSpec((bb, t, ck), row3)],
        out_specs=pl.BlockSpec((bb, t, c), row3),
        out_shape=jax.ShapeDtypeStruct((n, t, c), F32),
        compiler_params=_params("parallel"),
        name="sample_a_attn",
    )(sinks.astype(F32), q, kcache, knew, vcache, vnew)


def _sample_b_kernel(q_ref, kt_ref, kn_ref, vt_ref, vn_ref, o_ref, *, kvh):
    t = q_ref.shape[0]
    buf = kt_ref.shape[1]
    c = o_ref.shape[1]
    rows = GQA * t
    q = q_ref[...]
    kn = kn_ref[...].astype(BF16)
    vn = vn_ref[...].astype(BF16)
    tok_n = _mod(lax.broadcasted_iota(jnp.int32, (rows, t), 0), t)
    dist_n = tok_n - lax.broadcasted_iota(jnp.int32, (rows, t), 1)
    ones_c = jnp.ones((HEAD_DIM, buf), BF16)
    ones_n = jnp.ones((t, HEAD_DIM), BF16)
    nt = (((1,), (1,)), ((), ()))
    starts = [max(buf - window, 0) for window, _ in B_GROUPS]
    pairs = [(h, gi) for h in range(kvh) for gi in range(len(B_GROUPS))]
    hsl = lambda h: slice(h * HEAD_DIM, (h + 1) * HEAD_DIM)
    scores = []
    for h, gi in pairs:
        if gi == 0:
            kt_h = kt_ref[hsl(h), :].astype(BF16)
        q_h = _stack_heads(q, gi * (c // HEAD_DIM) + h * GQA, t).astype(BF16)
        scores.append((jnp.dot(q_h, kt_h[:, starts[gi]:], preferred_element_type=F32),
                       lax.dot_general(q_h, kn[:, hsl(h)], nt, preferred_element_type=F32)))
    masks = []
    for (window, d), start in zip(B_GROUPS, starts):
        span = buf - start
        tok = _mod(lax.broadcasted_iota(jnp.int32, (rows, span), 0), t)
        dist_c = buf + tok - (start + lax.broadcasted_iota(jnp.int32, (rows, span), 1))
        masks.append(((_mod(dist_c, d) == 0) & (dist_c <= window),
                      (dist_n >= 0) & (_mod(dist_n, d) == 0) & (dist_n <= window)))
    probs = []
    for (h, gi), (s_c, s_n) in zip(pairs, scores):
        s_c = jnp.where(masks[gi][0], s_c, NEG_INF)
        s_n = jnp.where(masks[gi][1], s_n, NEG_INF)
        m = jnp.maximum(jnp.max(s_c, axis=-1, keepdims=True), jnp.max(s_n, axis=-1, keepdims=True))
        probs.append((jnp.exp(s_c - m).astype(BF16), jnp.exp(s_n - m).astype(BF16), m))
    results = []
    for (h, gi), (e_c, e_n, m) in zip(pairs, probs):
        if gi == 0:
            vt_h = jnp.concatenate([vt_ref[hsl(h), :].astype(BF16), ones_c, ones_c], axis=0)
            vn_h = jnp.concatenate([vn[:, hsl(h)], ones_n, ones_n], axis=1)
        res = (lax.dot_general(e_c, vt_h[:, starts[gi]:], nt, preferred_element_type=F32)
               + jnp.dot(e_n, vn_h, preferred_element_type=F32))
        results.append((res, m))
    for h in range(kvh):
        outs, lses = [], []
        for res, m in results[h * len(B_GROUPS):(h + 1) * len(B_GROUPS)]:
            den = res[:, LANES:LANES + HEAD_DIM]
            outs.append(res[:, :HEAD_DIM] * (1.0 / den))
            lses.append(jnp.log(den) + m)
        top = functools.reduce(jnp.maximum, lses)
        ws = [jnp.exp(l - top) for l in lses]
        inv = 1.0 / functools.reduce(jnp.add, ws)
        o = functools.reduce(jnp.add, [o_g * (w * inv) for o_g, w in zip(outs, ws)])
        for g in range(GQA):
            col0 = (h * GQA + g) * HEAD_DIM
            o_ref[:, col0:col0 + HEAD_DIM] = o[g * t:(g + 1) * t].astype(o_ref.dtype)


def _sample_b_attn(q, kt_cache, knew, vt_cache, vnew):
    n, t, gc = q.shape
    c = gc // len(B_GROUPS)
    ck, buf = kt_cache.shape[1:]
    assert all(max(buf - w, 0) % LANES == 0 for w, _ in B_GROUPS)
    row3 = lambda b: (b, 0, 0)
    return pl.pallas_call(
        functools.partial(_sample_b_kernel, kvh=ck // HEAD_DIM),
        grid=(n,),
        in_specs=[pl.BlockSpec((None, t, gc), row3),
                  pl.BlockSpec((None, ck, buf), row3), pl.BlockSpec((None, t, ck), row3),
                  pl.BlockSpec((None, ck, buf), row3), pl.BlockSpec((None, t, ck), row3)],
        out_specs=pl.BlockSpec((None, t, c), row3),
        out_shape=jax.ShapeDtypeStruct((n, t, c), F32),
        compiler_params=_params("parallel"),
        name="sample_b_attn",
    )(q, kt_cache, knew, vt_cache, vnew)


def _rope_tables(pos):
    half = HEAD_DIM // 2
    inv = ROPE_THETA ** (-jnp.arange(half, dtype=F32) / half)
    ang = pos.astype(F32)[:, None] * inv[None, :]
    cos, sin = jnp.cos(ang), jnp.sin(ang)
    reps = LANES // HEAD_DIM
    return jnp.tile(jnp.concatenate([cos, cos], axis=1), (1, reps)), jnp.tile(jnp.concatenate([-sin, sin], axis=1), (1, reps))


def _trunk(x, pos, sample_caches, wts, w16):
    (ln_g, ln_b, w_qkv_a, sinks_a, w_o_a, w_kv_b, w_q_b, w_o_b, w_up, w_down) = wts
    n, t, dm = x.shape
    m = n * t
    depth = ln_g.shape[0]
    n_a = w_qkv_a.shape[0]
    alpha = (2 * depth) ** 0.25
    qa = w_o_a.shape[1]
    kva = (w_qkv_a.shape[2] - qa) // 2
    kvb = w_kv_b.shape[1] // 2
    cb = w_o_b.shape[1]
    ng = len(B_GROUPS)
    scale = HEAD_DIM ** -0.5
    is_prompt = sample_caches is None
    rope = _rope_tables(pos)
    if is_prompt:
        rope_by_d = {d: rope if d == 1 else _rope_tables(pos.reshape(t // d, d).T.reshape(t)) for _, d in B_GROUPS}
    if not is_prompt:
        rope = tuple(jnp.tile(r, (n, 1)) for r in rope)
    qdt = BF16 if is_prompt else F32

    def project(key, xin, w, layer, col0, ncols, out_dtype, full_key=None, side=None, side_key=None, **kw):
        if not is_prompt:
            side = None
        if full_key in w16:
            res = _proj(xin, w16[full_key], out_dtype, col0=col0, n=ncols, side=side, **kw)
            res = res if side is not None else (res,)
        elif not is_prompt:
            return _proj(xin, w16[key], out_dtype, **kw)
        elif key in w16:
            res = _proj(xin, w16[key], out_dtype, side=side, **kw)
            res = res if side is not None else (res,)
        else:
            res = _proj(xin, w, out_dtype, layer=layer, col0=col0, n=ncols, side=side, **kw)
            w16[key] = res[1]
        if side is not None:
            w16[side_key] = res[-1]
        return res[0]

    x32 = x.reshape(m, dm)
    x16 = x32
    a_states = []
    b_state = None
    b_kv = None
    for layer in range(depth):
        if layer < n_a:
            full = ("full", "qkv_a", layer)
            q = project(("q_a", layer), x16, w_qkv_a, layer, 0, qa, qdt, full_key=full, rope=rope, rope_cols=qa,
                        scale=scale, side=(w_o_a, layer), side_key=("o_a", layer))
            kv = project(("kv_a", layer), x16, w_qkv_a, layer, qa, 2 * kva, F32, full_key=full,
                         rope=rope, rope_cols=kva)
            kv3 = kv.reshape(n, t, 2 * kva)
            if is_prompt:
                hosted = {("up", layer): (w_up, layer)}
                if layer + 1 < n_a:
                    hosted[("full", "qkv_a", layer + 1)] = (w_qkv_a, layer + 1)
                elif layer + 1 < depth:
                    hosted[("full", "q_b", 0)] = (w_q_b, 0)
                    hosted[("full", "kv_b")] = (w_kv_b[None], 0)
                res = _band_attn(q.reshape(n, t, qa), kv3, A_WINDOW - 1, sinks=sinks_a[layer],
                                 sides=list(hosted.values()))
                w16.update(zip(hosted, res[1:]))
                o = res[0].reshape(m, qa)
                keep = min(A_WINDOW, t)
                a_states.append((kv3[:, -keep:, :kva], kv3[:, -keep:, kva:]))
            else:
                ck = sample_caches[0][layer].reshape(n, -1, kva)
                cv = sample_caches[1][layer].reshape(n, -1, kva)
                k3, v3 = kv3[:, :, :kva], kv3[:, :, kva:]
                o = _sample_a_attn(q.reshape(n, t, qa), ck, k3, cv, v3, sinks_a[layer]).reshape(m, qa)
                keep = ck.shape[1]
                a_states.append((jnp.concatenate([ck, k3], axis=1)[:, -keep:],
                                 jnp.concatenate([cv, v3], axis=1)[:, -keep:]))
            x32, x16 = _res_ln(o, w16[("o_a", layer)], x32, ln_g[layer, 0], ln_b[layer, 0], alpha)
        else:
            j = layer - n_a
            if j == 0:
                kv = project("kv_b", x16, w_kv_b, None, 0, 2 * kvb, F32, full_key=("full", "kv_b"),
                             rope=rope, rope_cols=kvb)
                kv3 = kv.reshape(n, t, 2 * kvb)
                if is_prompt:
                    keep = min(max(w for w, _ in B_GROUPS), t)
                    b_state = (kv3[:, -keep:, :kvb], kv3[:, -keep:, kvb:])
                    dils = [d for _, d in B_GROUPS if d > 1]
                    kv_res = dict(zip(dils, _by_residue(kv3, dils, F32)))
                    b_kv = [kv3 if d == 1 else kv_res[d].reshape(n * d, t // d, 2 * kvb) for _, d in B_GROUPS]
                else:
                    k3, v3 = kv3[:, :, :kvb], kv3[:, :, kvb:]
                    ck = sample_caches[2].reshape(n, -1, kvb)
                    cv = sample_caches[3].reshape(n, -1, kvb)
                    keep = ck.shape[1]
                    b_state = (jnp.concatenate([ck, k3], axis=1)[:, -keep:],
                               jnp.concatenate([cv, v3], axis=1)[:, -keep:])
                    b_kv = (jnp.swapaxes(ck, 1, 2), k3, jnp.swapaxes(cv, 1, 2), v3)
            if is_prompt:
                outs, lses = [], []
                dils = [d for _, d in B_GROUPS if d > 1]
                x_res = dict(zip(dils, _by_residue(x32.reshape(n, t, dm), dils, BF16)))
                for gi, (window, d) in enumerate(B_GROUPS):
                    xg = x16 if d == 1 else x_res[d].reshape(m, dm)
                    side = dict(side=(w_o_b, j), side_key=("o_b", j)) if gi == 0 else {}
                    q = project(("q_b", j, gi), xg, w_q_b, j, gi * cb, cb, BF16, full_key=("full", "q_b", j),
                                rope=rope_by_d[d], rope_cols=cb, scale=scale, **side)
                    hosted = {}
                    if gi == 0:
                        hosted[("up", layer)] = (w_up, layer)
                    elif gi == 1 and layer + 1 < depth:
                        hosted[("full", "q_b", j + 1)] = (w_q_b, j + 1)
                    res = _band_attn(q.reshape(n * d, t // d, cb), b_kv[gi], window // d, with_lse=True,
                                     sides=list(hosted.values()))
                    o_g, lse_g = res[:2]
                    w16.update(zip(hosted, res[2:]))
                    outs.append(o_g.reshape(n, d, t // d, cb))
                    lses.append(lse_g.reshape(n, d, t // d, LANES))
                o = _combine(outs, lses, [d for _, d in B_GROUPS]).reshape(m, cb)
            else:
                if ("full", "q_b", j) in w16:
                    q = project(None, x16, None, None, 0, ng * cb, qdt, full_key=("full", "q_b", j),
                                rope=rope, rope_cols=ng * cb, scale=scale)
                else:
                    q = jnp.concatenate([project(("q_b", j, gi), x16, None, None, 0, cb, qdt,
                                                 rope=rope, rope_cols=cb, scale=scale) for gi in range(ng)], axis=1)
                o = _sample_b_attn(q.reshape(n, t, ng * cb), *b_kv).reshape(m, cb)
            x32, x16 = _res_ln(o, w16[("o_b", j)], x32, ln_g[layer, 0], ln_b[layer, 0], alpha)
        hid = project(("up", layer), x16, w_up, layer, 0, w_up.shape[2], BF16, relu2=True,
                      side=(w_down, layer), side_key=("down", layer), tn=UP_TN)
        x32, x16 = _res_ln(hid, w16[("down", layer)], x32, ln_g[layer, 1], ln_b[layer, 1], alpha)
    return x32.reshape(n, t, dm), a_states, b_state


def kernel(x_prompt, x_sample, cache_a_k, cache_a_v, cache_b_k, cache_b_v, ln_g, ln_b,
           w_qkv_a, sinks_a, w_o_a, w_kv_b, w_q_b, w_o_b, w_up, w_down):
    wts = (ln_g, ln_b, w_qkv_a, sinks_a, w_o_a, w_kv_b, w_q_b, w_o_b, w_up, w_down)
    w16 = {}
    kv_heads_a = cache_a_k.shape[3]
    kv_heads_b = cache_b_k.shape[2]
    pos_prompt = jnp.arange(x_prompt.shape[1], dtype=jnp.int32)
    pos_sample = PAST_LEN + jnp.arange(x_sample.shape[1], dtype=jnp.int32)
    y_prompt, a_prompt, b_prompt = _trunk(x_prompt, pos_prompt, None, wts, w16)
    y_sample, a_sample, b_sample = _trunk(x_sample, pos_sample, (cache_a_k, cache_a_v, cache_b_k, cache_b_v), wts, w16)

    def heads(s, kvh):
        return s.reshape(*s.shape[:-1], kvh, HEAD_DIM)

    return (y_prompt, y_sample,
            jnp.stack([heads(s[0], kv_heads_a) for s in a_prompt], axis=0),
            jnp.stack([heads(s[1], kv_heads_a) for s in a_prompt], axis=0),
            jnp.stack([heads(s[0], kv_heads_a) for s in a_sample], axis=0),
            jnp.stack([heads(s[1], kv_heads_a) for s in a_sample], axis=0),
            heads(b_prompt[0], kv_heads_b), heads(b_prompt[1], kv_heads_b),
            heads(b_sample[0], kv_heads_b), heads(b_sample[1], kv_heads_b))
```

```python
import functools

import jax
import jax.numpy as jnp
from jax import lax
from jax.experimental import pallas as pl
from jax.experimental.pallas import tpu as pltpu

HEAD_DIM = 64
GQA = 8
A_WINDOW = 128
B_GROUPS = ((128, 1), (512, 4), (2048, 16))
PAST_LEN = 16384
ROPE_THETA = 10000.0
LN_EPS = 1e-5
BLOCK = 128
NEG_INF = -1e30
LANES = 128
V7X_VMEM_BYTES = 64 * 1024 * 1024
VMEM_LIMIT = V7X_VMEM_BYTES * 7 // 8

PROJ_TM, PROJ_TN = 1024, 1024
UP_TN = 2048
RES_TM, RES_TK = 512, 2048
WO_TM = 256
COMBINE_TM = 256
SAMPLE_A_SEQS = 8
LN_ROWS = 16

BF16 = jnp.bfloat16
F32 = jnp.float32


def _mod(x, n):
    assert n & (n - 1) == 0, "power-of-two modulus only"
    return x & (n - 1)


def _div(x, n):
    assert n & (n - 1) == 0, "power-of-two divisor only"
    return x >> (n.bit_length() - 1)


def _params(*sem):
    return pltpu.CompilerParams(dimension_semantics=sem, vmem_limit_bytes=VMEM_LIMIT)


def _pick(n, pref):
    if n <= pref:
        return n
    t = pref
    while n % t:
        t //= 2
    return t


def _rope_chunk(x, cos, sin_signed):
    lane = lax.broadcasted_iota(jnp.int32, x.shape, 1)
    first = _mod(lane, HEAD_DIM) < (HEAD_DIM // 2)
    partner = jnp.where(first, pltpu.roll(x, LANES - HEAD_DIM // 2, 1), pltpu.roll(x, HEAD_DIM // 2, 1))
    return x * cos + partner * sin_signed


def _proj_kernel(*refs, rope_chunks, scale, relu2, cast_w, has_side, delay_tiles):
    refs = list(refs)
    a_ref, w_ref = refs[:2]
    del refs[:2]
    cos_ref, sin_ref = (refs.pop(0), refs.pop(0)) if rope_chunks else (None, None)
    if has_side:
        side_ref = refs.pop(0)
    o_ref = refs.pop(0)
    if has_side:
        side_out_ref = refs.pop(1 if cast_w else 0)
        side_out_ref[...] = side_ref[...].astype(BF16)
    i = pl.program_id(1)
    if cast_w:
        w16_out_ref, w16_ref = refs[:2]

        @pl.when(i == 0)
        def _():
            w16 = w_ref[...].astype(BF16)
            w16_ref[...] = w16
            w16_out_ref[...] = w16

    def product():
        w = w16_ref[...] if cast_w else w_ref[...]
        return jnp.dot(a_ref[...].astype(BF16), w, preferred_element_type=F32)

    def finish(acc):
        cos = cos_ref[...]
        sin = sin_ref[...]
        for c in range(acc.shape[1] // LANES):
            x = acc[:, c * LANES:(c + 1) * LANES]
            if c < rope_chunks:
                x = _rope_chunk(x, cos, sin)
                if scale != 1.0:
                    x = x * scale
            o_ref[:, c * LANES:(c + 1) * LANES] = x.astype(o_ref.dtype)

    if delay_tiles:
        accs = refs[-2:]

        @pl.when(i == 0)
        def _():
            accs[0][...] = product()

        for parity in range(2):
            @pl.when((i > 0) & (i < delay_tiles) & (i % 2 == parity))
            def _():
                accs[parity][...] = product()
                finish(accs[1 - parity])

        @pl.when(i == delay_tiles)
        def _():
            finish(accs[(delay_tiles - 1) % 2])

        return
    acc = product()
    if relu2:
        acc = jnp.square(jnp.maximum(acc, 0.0))
    if not rope_chunks:
        o_ref[...] = acc.astype(o_ref.dtype)
        return
    finish(acc)


def _proj(a, w, out_dtype, *, layer=None, col0=0, n=None, rope=None, rope_cols=0, scale=1.0, relu2=False,
          side=None, tm=PROJ_TM, tn=PROJ_TN):
    m, k = a.shape
    n = w.shape[-1] if n is None else n
    if m <= tm // 2:
        tn = max(tn, 2 * PROJ_TN)
    elif a.dtype != BF16:
        tm //= 2
    tm = _pick(m, tm)
    tn = _pick(n, tn)
    assert col0 % tn == 0
    joff = col0 // tn
    cast_w = w.dtype != BF16
    rope_chunks = 0
    if rope is not None:
        cos, sin = rope
        p = cos.shape[0]
        tm = _pick(p, tm)
        pb = p // tm
        assert rope_cols == n or tn == n
        rope_chunks = min(rope_cols, tn) // LANES
    nj, ni = n // tn, m // tm
    delay = rope is not None and rope_chunks * LANES == tn and ni >= 4
    row = (lambda i: jnp.minimum(i, ni - 1)) if delay else (lambda i: i)
    done = (lambda i: jnp.maximum(i - 1, 0)) if delay else (lambda i: i)
    w_mode = dict(pipeline_mode=pl.Buffered(1)) if delay and cast_w else {}
    if w.ndim == 3:
        w_spec = pl.BlockSpec((None, k, tn), lambda j, i: (layer, 0, j + joff), **w_mode)
    else:
        w_spec = pl.BlockSpec((k, tn), lambda j, i: (0, j + joff), **w_mode)
    in_specs = [pl.BlockSpec((tm, k), lambda j, i: (row(i), 0)), w_spec]
    args = [a, w]
    if rope is not None:
        in_specs += [pl.BlockSpec((tm, LANES), lambda j, i: (done(i) % pb, 0))] * 2
        args += [cos, sin]
    out_specs = [pl.BlockSpec((tm, tn), lambda j, i: (done(i), j))]
    out_shape = [jax.ShapeDtypeStruct((m, n), out_dtype)]
    scratch = []
    if cast_w:
        out_specs.append(pl.BlockSpec((k, tn), lambda j, i: (0, j)))
        out_shape.append(jax.ShapeDtypeStruct((k, n), BF16))
        scratch.append(pltpu.VMEM((k, tn), BF16))
    if delay:
        scratch += [pltpu.VMEM((tm, tn), F32), pltpu.VMEM((tm, tn), F32)]
    if side is not None:
        w_side, side_layer = side
        _, rows, cols = w_side.shape
        rs = rows // (nj * ni)
        assert rs * nj * ni == rows and rs % 16 == 0
        in_specs.append(pl.BlockSpec((None, rs, cols), lambda j, i: (side_layer, j * ni + row(i), 0)))
        args.append(w_side)
        out_specs.append(pl.BlockSpec((rs, cols), lambda j, i: (j * ni + row(i), 0)))
        out_shape.append(jax.ShapeDtypeStruct((rows, cols), BF16))
    res = pl.pallas_call(
        functools.partial(_proj_kernel, rope_chunks=rope_chunks, scale=scale, relu2=relu2, cast_w=cast_w,
                          has_side=side is not None, delay_tiles=ni if delay else 0),
        grid=(nj, ni + 1 if delay else ni),
        in_specs=in_specs,
        out_specs=out_specs,
        out_shape=out_shape,
        scratch_shapes=scratch,
        compiler_params=_params("parallel", "arbitrary"),
        name="proj",
    )(*args)
    return tuple(res) if len(res) > 1 else res[0]


def _res_ln_kernel(a_ref, w_ref, x_ref, g_ref, b_ref, o32_ref, o16_ref, acc0_ref, acc1_ref, *, alpha, mt, nk):
    i = pl.program_id(0)
    kk = pl.program_id(1)
    accs = (acc0_ref, acc1_ref)
    rc = acc0_ref.shape[0] // nk

    strips = rc // LN_ROWS
    n = acc0_ref.shape[1]

    def matmul(acc_ref, part=0, parts=1):
        cols = slice(part * n // parts, (part + 1) * n // parts)
        part_product = jnp.dot(a_ref[...].astype(BF16), w_ref[:, cols], preferred_element_type=F32)
        if nk == 1:
            acc_ref[:, cols] = part_product
        else:
            acc_ref[:, cols] += part_product

    def layernorm(acc_ref, part=0, parts=1):
        gain = g_ref[...]
        bias = b_ref[...]
        for s in range(part * strips // parts, (part + 1) * strips // parts):
            if nk == 1:
                rows = slice(s * LN_ROWS, (s + 1) * LN_ROWS)
            else:
                rows = pl.ds(pl.multiple_of(kk * rc + s * LN_ROWS, LN_ROWS), LN_ROWS)
            v = alpha * x_ref[rows, :] + acc_ref[rows, :]
            if nk > 1:
                acc_ref[rows, :] = jnp.zeros((LN_ROWS, acc_ref.shape[1]), F32)
            mu = jnp.mean(v, axis=-1, keepdims=True)
            c = v - mu
            var = jnp.mean(c * c, axis=-1, keepdims=True)
            y = c * lax.rsqrt(var + LN_EPS) * gain + bias
            o32_ref[rows, :] = y
            o16_ref[rows, :] = y.astype(BF16)

    @pl.when((i == 0) & (kk == 0))
    def _():
        acc0_ref[...] = jnp.zeros_like(acc0_ref)
        acc1_ref[...] = jnp.zeros_like(acc1_ref)

    @pl.when(i == 0)
    def _():
        matmul(acc0_ref)

    for parity in range(2):
        @pl.when((i > 0) & (i < mt) & (i % 2 == parity))
        def _():
            matmul(accs[parity])
            layernorm(accs[1 - parity])

    @pl.when(i == mt)
    def _():
        layernorm(accs[(mt - 1) % 2])


def _res_ln(a, w, x, g, b, alpha, *, tm=RES_TM, tk=RES_TK):
    m, k = a.shape
    n = w.shape[1]
    tm = _pick(m, tm)
    tk = _pick(k, tk)
    while tk < k and (tm // (k // tk)) % LN_ROWS:
        tk *= 2
    mt, nk = m // tm, k // tk
    assert (tm // nk) % LN_ROWS == 0 and tm % nk == 0
    a_map = lambda i, kk: (jnp.minimum(i, mt - 1), jnp.where(i < mt, kk, nk - 1))
    w_map = lambda i, kk: (jnp.where(i < mt, kk, nk - 1), 0)
    prev_map = lambda i, kk: (jnp.maximum(i - 1, 0), 0)
    return pl.pallas_call(
        functools.partial(_res_ln_kernel, alpha=alpha, mt=mt, nk=nk),
        grid=(mt + 1, nk),
        in_specs=[
            pl.BlockSpec((tm, tk), a_map),
            pl.BlockSpec((tk, n), w_map),
            pl.BlockSpec((tm, n), prev_map),
            pl.BlockSpec((1, n), lambda i, kk: (0, 0)),
            pl.BlockSpec((1, n), lambda i, kk: (0, 0)),
        ],
        out_specs=[pl.BlockSpec((tm, n), prev_map), pl.BlockSpec((tm, n), prev_map)],
        out_shape=[jax.ShapeDtypeStruct((m, n), F32), jax.ShapeDtypeStruct((m, n), BF16)],
        scratch_shapes=[pltpu.VMEM((tm, n), F32), pltpu.VMEM((tm, n), F32)],
        compiler_params=_params("arbitrary", "arbitrary"),
        name="res_ln",
    )(a, w, x, g.reshape(1, n), b.reshape(1, n))


def _band_kernel(*refs, n_back, kvh, has_sink, with_lse, qb, n_sides):
    refs = list(refs)
    sink_ref = refs.pop(0) if has_sink else None
    side_out_refs = [refs.pop() for _ in range(n_sides)][::-1]
    side_refs = [refs.pop(5) for _ in range(n_sides)]
    for side_ref, side_out_ref in zip(side_refs, side_out_refs):
        side_out_ref[...] = side_ref[...].astype(BF16)
    q_ref, kp_ref, kc_ref, vp_ref, vc_ref, o_ref = refs[:6]
    lse_ref = refs[6] if with_lse else None
    i = pl.program_id(1)
    k_all = jnp.concatenate([kp_ref[...], kc_ref[...]], axis=0).astype(BF16)
    v_all = jnp.concatenate([vp_ref[...], vc_ref[...]], axis=0).astype(BF16)
    qi = lax.broadcasted_iota(jnp.int32, (BLOCK, 2 * BLOCK), 0)
    ki = lax.broadcasted_iota(jnp.int32, (BLOCK, 2 * BLOCK), 1)
    dist = BLOCK + qi - ki
    in_band = (dist >= 0) & (dist <= n_back)
    lane = lax.broadcasted_iota(jnp.int32, (BLOCK, LANES), 1)
    low = lane < HEAD_DIM
    if has_sink:
        assert n_back < BLOCK
        sink_col = ki == 0
        not_row0 = lax.broadcasted_iota(jnp.int32, (2 * BLOCK, HEAD_DIM), 0) > 0
    zero = jnp.zeros((2 * BLOCK, HEAD_DIM), BF16)
    one = jnp.ones((2 * BLOCK, HEAD_DIM), BF16)
    nt = (((1,), (1,)), ((), ()))
    npair = GQA // 2
    for sb in range(qb):
        rows = slice(sb * BLOCK, (sb + 1) * BLOCK)
        q = q_ref[rows, :]
        kk = k_all[sb * BLOCK:(sb + 2) * BLOCK]
        vv = v_all[sb * BLOCK:(sb + 2) * BLOCK]
        valid = in_band & ((ki >= BLOCK) | (i > 0)) if sb == 0 else in_band
        _band_block(q, kk, vv, valid, sink_ref, o_ref.at[rows], lse_ref.at[rows] if with_lse else None,
                    kvh=kvh, consts=(zero, one, nt, npair, lane, low, sink_col if has_sink else None,
                                     not_row0 if has_sink else None))


def _band_block(q, kk, vv, valid, sink_ref, o_ref, lse_ref, *, kvh, consts):
    zero, one, nt, npair, lane, low, sink_col, not_row0 = consts
    has_sink = sink_ref is not None
    with_lse = lse_ref is not None
    lse_tile = jnp.zeros((BLOCK, LANES), F32)
    scores, maxes, probs, results = {}, {}, {}, {}

    def score(h):
        k_h = kk[:, h * HEAD_DIM:(h + 1) * HEAD_DIM]
        k_sel = (jnp.concatenate([k_h, zero], axis=1), jnp.concatenate([zero, k_h], axis=1))
        q_pairs = jnp.concatenate(
            [q[:, (h * GQA + 2 * j) * HEAD_DIM:(h * GQA + 2 * j + 2) * HEAD_DIM] for j in range(npair)], axis=0)
        scores[h] = [lax.dot_general(q_pairs, k_sel[par], nt, preferred_element_type=F32) for par in range(2)]

    def softmax(h):
        maxes[h], probs[h] = [], []
        for par in range(2):
            es = []
            for j in range(npair):
                fill = NEG_INF
                if has_sink:
                    fill = jnp.where(sink_col, sink_ref[h * GQA + 2 * j + par], NEG_INF)
                s_g = jnp.where(valid, scores[h][par][j * BLOCK:(j + 1) * BLOCK], fill)
                m = jnp.max(s_g, axis=-1, keepdims=True)
                es.append(jnp.exp(s_g - m).astype(BF16))
                maxes[h].append(m)
            probs[h].append(jnp.concatenate(es, axis=0))

    def values(h):
        v_h = vv[:, h * HEAD_DIM:(h + 1) * HEAD_DIM]
        if has_sink:
            v_h = jnp.where(not_row0, v_h, zero)
        v_sel = (jnp.concatenate([v_h, zero, one, zero], axis=1),
                 jnp.concatenate([zero, v_h, zero, one], axis=1))
        results[h] = (jnp.dot(probs[h][0], v_sel[0], preferred_element_type=F32)
                      + jnp.dot(probs[h][1], v_sel[1], preferred_element_type=F32))

    for step in range(kvh + 2):
        if step < kvh:
            score(step)
        if 0 <= step - 1 < kvh:
            softmax(step - 1)
        if 0 <= step - 2 < kvh:
            values(step - 2)
    for h in range(kvh):
        for j in range(npair):
            blk = results[h][j * BLOCK:(j + 1) * BLOCK]
            den = blk[:, LANES:]
            col = (h * GQA + 2 * j) * HEAD_DIM
            o_ref[:, col:col + LANES] = (blk[:, :LANES] * (1.0 / den)).astype(o_ref.dtype)
            if with_lse:
                m_pair = jnp.where(low, maxes[h][j], maxes[h][npair + j])
                pair_id = h * npair + j
                lse_tile = jnp.where(_mod(lane, HEAD_DIM) == pair_id, jnp.log(den) + m_pair, lse_tile)
    if with_lse:
        lse_ref[...] = lse_tile


def _band_attn(q, kv, n_back, sinks=None, with_lse=False, sides=()):
    n, length, c = q.shape
    ck = kv.shape[2] // 2
    kvh = ck // HEAD_DIM
    assert c == kvh * GQA * HEAD_DIM and length % BLOCK == 0 and n_back <= BLOCK
    qb = 2 if length % (2 * BLOCK) == 0 else 1
    nb = length // (qb * BLOCK)
    cur = lambda b, i: (b, i, 0)
    cur_at = lambda col: (lambda b, i: (b, i, col))
    prev_at = lambda col: (lambda b, i: (b, jnp.maximum(qb * i - 1, 0), col))
    in_specs = [
        pl.BlockSpec((None, qb * BLOCK, c), cur),
        pl.BlockSpec((None, BLOCK, ck), prev_at(0)), pl.BlockSpec((None, qb * BLOCK, ck), cur_at(0)),
        pl.BlockSpec((None, BLOCK, ck), prev_at(1)), pl.BlockSpec((None, qb * BLOCK, ck), cur_at(1)),
    ]
    args = [q, kv, kv, kv, kv]
    out_specs = [pl.BlockSpec((None, qb * BLOCK, c), cur)]
    out_shape = [jax.ShapeDtypeStruct(q.shape, BF16)]
    if with_lse:
        out_specs.append(pl.BlockSpec((None, qb * BLOCK, LANES), cur))
        out_shape.append(jax.ShapeDtypeStruct((n, length, LANES), F32))
    for w_side, side_layer in sides:
        _, rows, cols = w_side.shape
        rs = rows // (n * nb)
        assert rs * n * nb == rows and rs % 16 == 0
        in_specs.append(pl.BlockSpec((None, rs, cols), lambda b, i, side_layer=side_layer: (side_layer, b * nb + i, 0)))
        args.append(w_side)
        out_specs.append(pl.BlockSpec((rs, cols), lambda b, i: (b * nb + i, 0)))
        out_shape.append(jax.ShapeDtypeStruct((rows, cols), BF16))
    if sinks is not None:
        in_specs = [pl.BlockSpec(memory_space=pltpu.SMEM)] + in_specs
        args = [sinks.astype(F32)] + args
    res = pl.pallas_call(
        functools.partial(_band_kernel, n_back=n_back, kvh=kvh, has_sink=sinks is not None, with_lse=with_lse,
                          qb=qb, n_sides=len(sides)),
        grid=(n, nb),
        in_specs=in_specs,
        out_specs=out_specs,
        out_shape=out_shape,
        compiler_params=_params("parallel", "arbitrary"),
        name="band_attn",
    )(*args)
    return res if len(res) > 1 else res[0]


def _permute_kernel(x_ref, *o_refs):
    for o_ref in o_refs:
        d = o_ref.shape[0]
        rows = o_ref.shape[1]
        for r in range(d):
            o_ref[r] = x_ref[pl.ds(r, rows, stride=d), :].astype(o_ref.dtype)


def _by_residue(x, dils, out_dtype):
    n, t, c = x.shape
    return pl.pallas_call(
        _permute_kernel,
        grid=(n, c // LANES),
        in_specs=[pl.BlockSpec((None, t, LANES), lambda b, j: (b, 0, j))],
        out_specs=[pl.BlockSpec((None, d, t // d, LANES), lambda b, j: (b, 0, 0, j)) for d in dils],
        out_shape=[jax.ShapeDtypeStruct((n, d, t // d, c), out_dtype) for d in dils],
        compiler_params=_params("parallel", "parallel"),
        name="by_residue",
    )(x)


def _combine_kernel(*refs, dils):
    ng = len(dils)
    o_refs, l_refs, out_ref = refs[:ng], refs[ng:2 * ng], refs[2 * ng]
    stage = list(refs[2 * ng + 1:])
    c = out_ref.shape[1]
    nchunk = c // LANES
    outs, lses = [], []
    for o_ref, l_ref, d in zip(o_refs, l_refs, dils):
        if d == 1:
            outs.append(lambda j, o_ref=o_ref: o_ref[0, :, j * LANES:(j + 1) * LANES].astype(F32))
            lses.append(l_ref[0])
            continue
        o_st, l_st = stage.pop(0), stage.pop(0)
        rows = o_ref.shape[1]
        for r in range(d):
            l_st[pl.ds(r, rows, stride=d), :] = l_ref[r]
            for j in range(nchunk):
                o_st[j, pl.ds(r, rows, stride=d), :] = o_ref[r, :, j * LANES:(j + 1) * LANES].astype(F32)
        outs.append(lambda j, o_st=o_st: o_st[j])
        lses.append(l_st[...])
    top = functools.reduce(jnp.maximum, lses)
    ws = [jnp.exp(l - top) for l in lses]
    inv = 1.0 / functools.reduce(jnp.add, ws)
    split = []
    for w in ws[:-1]:
        w = w * inv
        hi = w.astype(BF16)
        split.append(jnp.concatenate([hi, (w - hi.astype(F32)).astype(BF16)], axis=1))
    assert nchunk % 2 == 0
    for jj in range(nchunk // 2):
        head_of_col = _div(lax.broadcasted_iota(jnp.int32, (2 * LANES, 2 * LANES), 1) + jj * 2 * LANES, HEAD_DIM)
        lane_of_col = _mod(head_of_col, 2) * HEAD_DIM + _div(head_of_col, 2)
        row = _mod(lax.broadcasted_iota(jnp.int32, (2 * LANES, 2 * LANES), 0), LANES)
        expand = (lane_of_col == row).astype(BF16)
        wides = [jnp.dot(w2, expand, preferred_element_type=F32) for w2 in split]
        for half in range(2):
            j = 2 * jj + half
            cols = slice(half * LANES, (half + 1) * LANES)
            acc, others = None, None
            for o, wide in zip(outs[:-1], wides):
                term = wide[:, cols] * o(j)
                acc = term if acc is None else acc + term
                others = wide[:, cols] if others is None else others + wide[:, cols]
            acc = acc + (1.0 - others) * outs[-1](j)
            out_ref[:, j * LANES:(j + 1) * LANES] = acc.astype(out_ref.dtype)


def _combine(outs, lses, dils, tm=COMBINE_TM):
    n, _, _, c = outs[0].shape
    t = outs[0].shape[1] * outs[0].shape[2]
    tm = _pick(t, tm)
    assert all(tm % (16 * d) == 0 for d in dils)
    blk = lambda width: [pl.BlockSpec((None, d, tm // d, width), lambda b, i: (b, 0, i, 0)) for d in dils]
    scratch = []
    for d in dils:
        if d > 1:
            scratch += [pltpu.VMEM((c // LANES, tm, LANES), F32), pltpu.VMEM((tm, LANES), F32)]
    return pl.pallas_call(
        functools.partial(_combine_kernel, dils=tuple(dils)),
        grid=(n, t // tm),
        in_specs=blk(c) + blk(LANES),
        out_specs=pl.BlockSpec((None, tm, c), lambda b, i: (b, i, 0)),
        out_shape=jax.ShapeDtypeStruct((n, t, c), BF16),
        scratch_shapes=scratch,
        compiler_params=_params("parallel", "parallel"),
        name="combine",
    )(*outs, *lses)


def _stack_heads(q, base, t):
    return jnp.concatenate([q[:, (base + g) * HEAD_DIM:(base + g + 1) * HEAD_DIM] for g in range(GQA)], axis=0)


def _sample_a_kernel(sink_ref, q_ref, kc_ref, kn_ref, vc_ref, vn_ref, o_ref, *, kvh, window):
    for b in range(q_ref.shape[0]):
        _sample_a_one(sink_ref, q_ref.at[b], kc_ref.at[b], kn_ref.at[b], vc_ref.at[b], vn_ref.at[b], o_ref.at[b],
                      kvh=kvh, window=window)


def _sample_a_one(sink_ref, q_ref, kc_ref, kn_ref, vc_ref, vn_ref, o_ref, *, kvh, window):
    t = q_ref.shape[0]
    buf = kc_ref.shape[0]
    q = q_ref[...]
    kc = jnp.concatenate([kc_ref[...], kn_ref[...]], axis=0).astype(BF16)
    vc = jnp.concatenate([vc_ref[...], vn_ref[...]], axis=0).astype(BF16)
    rows = GQA * t
    row = lax.broadcasted_iota(jnp.int32, (rows, buf + t), 0)
    col = lax.broadcasted_iota(jnp.int32, (rows, buf + t), 1)
    dist = buf + _mod(row, t) - col
    valid = (dist >= 0) & (dist < window)
    rowg = _div(lax.broadcasted_iota(jnp.int32, (rows, 1), 0), t)
    scores = [lax.dot_general(_stack_heads(q, h * GQA, t).astype(BF16), kc[:, h * HEAD_DIM:(h + 1) * HEAD_DIM],
                              (((1,), (1,)), ((), ())), preferred_element_type=F32) for h in range(kvh)]
    probs = []
    for h in range(kvh):
        s = jnp.where(valid, scores[h], NEG_INF)
        sink = jnp.zeros((rows, 1), F32)
        for g in range(GQA):
            sink = jnp.where(rowg == g, sink_ref[h * GQA + g], sink)
        m = jnp.maximum(jnp.max(s, axis=-1, keepdims=True), sink)
        e = jnp.exp(s - m)
        den = jnp.sum(e, axis=-1, keepdims=True) + jnp.exp(sink - m)
        probs.append((e.astype(BF16), 1.0 / den))
    for h in range(kvh):
        e, inv = probs[h]
        o = jnp.dot(e, vc[:, h * HEAD_DIM:(h + 1) * HEAD_DIM], preferred_element_type=F32) * inv
        for g in range(GQA):
            col0 = (h * GQA + g) * HEAD_DIM
            o_ref[:, col0:col0 + HEAD_DIM] = o[g * t:(g + 1) * t].astype(o_ref.dtype)


def _sample_a_attn(q, kcache, knew, vcache, vnew, sinks):
    n, t, c = q.shape
    buf, ck = kcache.shape[1:]
    bb = _pick(n, SAMPLE_A_SEQS)
    row3 = lambda b: (b, 0, 0)
    return pl.pallas_call(
        functools.partial(_sample_a_kernel, kvh=ck // HEAD_DIM, window=A_WINDOW),
        grid=(n // bb,),
        in_specs=[pl.BlockSpec(memory_space=pltpu.SMEM),
                  pl.BlockSpec((bb, t, c), row3),
                  pl.BlockSpec((bb, buf, ck), row3), pl.BlockSpec((bb, t, ck), row3),
                  pl.BlockSpec((bb, buf, ck), row3), pl.BlockSpec((bb, t, ck), row3)],
        out_specs=pl.BlockSpec((bb, t, c), row3),
        out_shape=jax.ShapeDtypeStruct((n, t, c), F32),
        compiler_params=_params("parallel"),
        name="sample_a_attn",
    )(sinks.astype(F32), q, kcache, knew, vcache, vnew)


def _sample_b_kernel(q_ref, kt_ref, kn_ref, vt_ref, vn_ref, o_ref, *, kvh):
    t = q_ref.shape[0]
    buf = kt_ref.shape[1]
    c = o_ref.shape[1]
    rows = GQA * t
    q = q_ref[...]
    kn = kn_ref[...].astype(BF16)
    vn = vn_ref[...].astype(BF16)
    tok_n = _mod(lax.broadcasted_iota(jnp.int32, (rows, t), 0), t)
    dist_n = tok_n - lax.broadcasted_iota(jnp.int32, (rows, t), 1)
    ones_c = jnp.ones((HEAD_DIM, buf), BF16)
    ones_n = jnp.ones((t, HEAD_DIM), BF16)
    nt = (((1,), (1,)), ((), ()))
    starts = [max(buf - window, 0) for window, _ in B_GROUPS]
    pairs = [(h, gi) for h in range(kvh) for gi in range(len(B_GROUPS))]
    hsl = lambda h: slice(h * HEAD_DIM, (h + 1) * HEAD_DIM)
    scores = []
    for h, gi in pairs:
        if gi == 0:
            kt_h = kt_ref[hsl(h), :].astype(BF16)
        q_h = _stack_heads(q, gi * (c // HEAD_DIM) + h * GQA, t).astype(BF16)
        scores.append((jnp.dot(q_h, kt_h[:, starts[gi]:], preferred_element_type=F32),
                       lax.dot_general(q_h, kn[:, hsl(h)], nt, preferred_element_type=F32)))
    masks = []
    for (window, d), start in zip(B_GROUPS, starts):
        span = buf - start
        tok = _mod(lax.broadcasted_iota(jnp.int32, (rows, span), 0), t)
        dist_c = buf + tok - (start + lax.broadcasted_iota(jnp.int32, (rows, span), 1))
        masks.append(((_mod(dist_c, d) == 0) & (dist_c <= window),
                      (dist_n >= 0) & (_mod(dist_n, d) == 0) & (dist_n <= window)))
    probs = []
    for (h, gi), (s_c, s_n) in zip(pairs, scores):
        s_c = jnp.where(masks[gi][0], s_c, NEG_INF)
        s_n = jnp.where(masks[gi][1], s_n, NEG_INF)
        m = jnp.maximum(jnp.max(s_c, axis=-1, keepdims=True), jnp.max(s_n, axis=-1, keepdims=True))
        probs.append((jnp.exp(s_c - m).astype(BF16), jnp.exp(s_n - m).astype(BF16), m))
    results = []
    for (h, gi), (e_c, e_n, m) in zip(pairs, probs):
        if gi == 0:
            vt_h = jnp.concatenate([vt_ref[hsl(h), :].astype(BF16), ones_c, ones_c], axis=0)
            vn_h = jnp.concatenate([vn[:, hsl(h)], ones_n, ones_n], axis=1)
        res = (lax.dot_general(e_c, vt_h[:, starts[gi]:], nt, preferred_element_type=F32)
               + jnp.dot(e_n, vn_h, preferred_element_type=F32))
        results.append((res, m))
    for h in range(kvh):
        outs, lses = [], []
        for res, m in results[h * len(B_GROUPS):(h + 1) * len(B_GROUPS)]:
            den = res[:, LANES:LANES + HEAD_DIM]
            outs.append(res[:, :HEAD_DIM] * (1.0 / den))
            lses.append(jnp.log(den) + m)
        top = functools.reduce(jnp.maximum, lses)
        ws = [jnp.exp(l - top) for l in lses]
        inv = 1.0 / functools.reduce(jnp.add, ws)
        o = functools.reduce(jnp.add, [o_g * (w * inv) for o_g, w in zip(outs, ws)])
        for g in range(GQA):
            col0 = (h * GQA + g) * HEAD_DIM
            o_ref[:, col0:col0 + HEAD_DIM] = o[g * t:(g + 1) * t].astype(o_ref.dtype)


def _sample_b_attn(q, kt_cache, knew, vt_cache, vnew):
    n, t, gc = q.shape
    c = gc // len(B_GROUPS)
    ck, buf = kt_cache.shape[1:]
    assert all(max(buf - w, 0) % LANES == 0 for w, _ in B_GROUPS)
    row3 = lambda b: (b, 0, 0)
    return pl.pallas_call(
        functools.partial(_sample_b_kernel, kvh=ck // HEAD_DIM),
        grid=(n,),
        in_specs=[pl.BlockSpec((None, t, gc), row3),
                  pl.BlockSpec((None, ck, buf), row3), pl.BlockSpec((None, t, ck), row3),
                  pl.BlockSpec((None, ck, buf), row3), pl.BlockSpec((None, t, ck), row3)],
        out_specs=pl.BlockSpec((None, t, c), row3),
        out_shape=jax.ShapeDtypeStruct((n, t, c), F32),
        compiler_params=_params("parallel"),
        name="sample_b_attn",
    )(q, kt_cache, knew, vt_cache, vnew)


def _rope_tables(pos):
    half = HEAD_DIM // 2
    inv = ROPE_THETA ** (-jnp.arange(half, dtype=F32) / half)
    ang = pos.astype(F32)[:, None] * inv[None, :]
    cos, sin = jnp.cos(ang), jnp.sin(ang)
    reps = LANES // HEAD_DIM
    return jnp.tile(jnp.concatenate([cos, cos], axis=1), (1, reps)), jnp.tile(jnp.concatenate([-sin, sin], axis=1), (1, reps))


def _trunk(x, pos, sample_caches, wts, w16):
    (ln_g, ln_b, w_qkv_a, sinks_a, w_o_a, w_kv_b, w_q_b, w_o_b, w_up, w_down) = wts
    n, t, dm = x.shape
    m = n * t
    depth = ln_g.shape[0]
    n_a = w_qkv_a.shape[0]
    alpha = (2 * depth) ** 0.25
    qa = w_o_a.shape[1]
    kva = (w_qkv_a.shape[2] - qa) // 2
    kvb = w_kv_b.shape[1] // 2
    cb = w_o_b.shape[1]
    ng = len(B_GROUPS)
    scale = HEAD_DIM ** -0.5
    is_prompt = sample_caches is None
    rope = _rope_tables(pos)
    if is_prompt:
        rope_by_d = {d: rope if d == 1 else _rope_tables(pos.reshape(t // d, d).T.reshape(t)) for _, d in B_GROUPS}
    if not is_prompt:
        rope = tuple(jnp.tile(r, (n, 1)) for r in rope)
    qdt = BF16 if is_prompt else F32

    def project(key, xin, w, layer, col0, ncols, out_dtype, full_key=None, side=None, side_key=None, **kw):
        if not is_prompt:
            side = None
        if full_key in w16:
            res = _proj(xin, w16[full_key], out_dtype, col0=col0, n=ncols, side=side, **kw)
            res = res if side is not None else (res,)
        elif not is_prompt:
            return _proj(xin, w16[key], out_dtype, **kw)
        elif key in w16:
            res = _proj(xin, w16[key], out_dtype, side=side, **kw)
            res = res if side is not None else (res,)
        else:
            res = _proj(xin, w, out_dtype, layer=layer, col0=col0, n=ncols, side=side, **kw)
            w16[key] = res[1]
        if side is not None:
            w16[side_key] = res[-1]
        return res[0]

    x32 = x.reshape(m, dm)
    x16 = x32
    a_states = []
    b_state = None
    b_kv = None
    for layer in range(depth):
        if layer < n_a:
            full = ("full", "qkv_a", layer)
            q = project(("q_a", layer), x16, w_qkv_a, layer, 0, qa, qdt, full_key=full, rope=rope, rope_cols=qa,
                        scale=scale, side=(w_o_a, layer), side_key=("o_a", layer))
            kv = project(("kv_a", layer), x16, w_qkv_a, layer, qa, 2 * kva, F32, full_key=full,
                         rope=rope, rope_cols=kva)
            kv3 = kv.reshape(n, t, 2 * kva)
            if is_prompt:
                hosted = {("up", layer): (w_up, layer)}
                if layer + 1 < n_a:
                    hosted[("full", "qkv_a", layer + 1)] = (w_qkv_a, layer + 1)
                elif layer + 1 < depth:
                    hosted[("full", "q_b", 0)] = (w_q_b, 0)
                    hosted[("full", "kv_b")] = (w_kv_b[None], 0)
                res = _band_attn(q.reshape(n, t, qa), kv3, A_WINDOW - 1, sinks=sinks_a[layer],
                                 sides=list(hosted.values()))
                w16.update(zip(hosted, res[1:]))
                o = res[0].reshape(m, qa)
                keep = min(A_WINDOW, t)
                a_states.append((kv3[:, -keep:, :kva], kv3[:, -keep:, kva:]))
            else:
                ck = sample_caches[0][layer].reshape(n, -1, kva)
                cv = sample_caches[1][layer].reshape(n, -1, kva)
                k3, v3 = kv3[:, :, :kva], kv3[:, :, kva:]
                o = _sample_a_attn(q.reshape(n, t, qa), ck, k3, cv, v3, sinks_a[layer]).reshape(m, qa)
                keep = ck.shape[1]
                a_states.append((jnp.concatenate([ck, k3], axis=1)[:, -keep:],
                                 jnp.concatenate([cv, v3], axis=1)[:, -keep:]))
            x32, x16 = _res_ln(o, w16[("o_a", layer)], x32, ln_g[layer, 0], ln_b[layer, 0], alpha, tm=WO_TM)
        else:
            j = layer - n_a
            if j == 0:
                kv = project("kv_b", x16, w_kv_b, None, 0, 2 * kvb, F32, full_key=("full", "kv_b"),
                             rope=rope, rope_cols=kvb)
                kv3 = kv.reshape(n, t, 2 * kvb)
                if is_prompt:
                    keep = min(max(w for w, _ in B_GROUPS), t)
                    b_state = (kv3[:, -keep:, :kvb], kv3[:, -keep:, kvb:])
                    dils = [d for _, d in B_GROUPS if d > 1]
                    kv_res = dict(zip(dils, _by_residue(kv3, dils, F32)))
                    b_kv = [kv3 if d == 1 else kv_res[d].reshape(n * d, t // d, 2 * kvb) for _, d in B_GROUPS]
                else:
                    k3, v3 = kv3[:, :, :kvb], kv3[:, :, kvb:]
                    ck = sample_caches[2].reshape(n, -1, kvb)
                    cv = sample_caches[3].reshape(n, -1, kvb)
                    keep = ck.shape[1]
                    b_state = (jnp.concatenate([ck, k3], axis=1)[:, -keep:],
                               jnp.concatenate([cv, v3], axis=1)[:, -keep:])
                    b_kv = (jnp.swapaxes(ck, 1, 2), k3, jnp.swapaxes(cv, 1, 2), v3)
            if is_prompt:
                outs, lses = [], []
                dils = [d for _, d in B_GROUPS if d > 1]
                x_res = dict(zip(dils, _by_residue(x32.reshape(n, t, dm), dils, BF16)))
                for gi, (window, d) in enumerate(B_GROUPS):
                    xg = x16 if d == 1 else x_res[d].reshape(m, dm)
                    side = dict(side=(w_o_b, j), side_key=("o_b", j)) if gi == 0 else {}
                    q = project(("q_b", j, gi), xg, w_q_b, j, gi * cb, cb, BF16, full_key=("full", "q_b", j),
                                rope=rope_by_d[d], rope_cols=cb, scale=scale, **side)
                    hosted = {}
                    if gi == 0:
                        hosted[("up", layer)] = (w_up, layer)
                    elif gi == 1 and layer + 1 < depth:
                        hosted[("full", "q_b", j + 1)] = (w_q_b, j + 1)
                    res = _band_attn(q.reshape(n * d, t // d, cb), b_kv[gi], window // d, with_lse=True,
                                     sides=list(hosted.values()))
                    o_g, lse_g = res[:2]
                    w16.update(zip(hosted, res[2:]))
                    outs.append(o_g.reshape(n, d, t // d, cb))
                    lses.append(lse_g.reshape(n, d, t // d, LANES))
                o = _combine(outs, lses, [d for _, d in B_GROUPS]).reshape(m, cb)
            else:
                if ("full", "q_b", j) in w16:
                    q = project(None, x16, None, None, 0, ng * cb, qdt, full_key=("full", "q_b", j),
                                rope=rope, rope_cols=ng * cb, scale=scale)
                else:
                    q = jnp.concatenate([project(("q_b", j, gi), x16, None, None, 0, cb, qdt,
                                                 rope=rope, rope_cols=cb, scale=scale) for gi in range(ng)], axis=1)
                o = _sample_b_attn(q.reshape(n, t, ng * cb), *b_kv).reshape(m, cb)
            x32, x16 = _res_ln(o, w16[("o_b", j)], x32, ln_g[layer, 0], ln_b[layer, 0], alpha, tm=WO_TM)
        hid = project(("up", layer), x16, w_up, layer, 0, w_up.shape[2], BF16, relu2=True,
                      side=(w_down, layer), side_key=("down", layer), tn=UP_TN)
        x32, x16 = _res_ln(hid, w16[("down", layer)], x32, ln_g[layer, 1], ln_b[layer, 1], alpha)
    return x32.reshape(n, t, dm), a_states, b_state


def kernel(x_prompt, x_sample, cache_a_k, cache_a_v, cache_b_k, cache_b_v, ln_g, ln_b,
           w_qkv_a, sinks_a, w_o_a, w_kv_b, w_q_b, w_o_b, w_up, w_down):
    wts = (ln_g, ln_b, w_qkv_a, sinks_a, w_o_a, w_kv_b, w_q_b, w_o_b, w_up, w_down)
    w16 = {}
    kv_heads_a = cache_a_k.shape[3]
    kv_heads_b = cache_b_k.shape[2]
    pos_prompt = jnp.arange(x_prompt.shape[1], dtype=jnp.int32)
    pos_sample = PAST_LEN + jnp.arange(x_sample.shape[1], dtype=jnp.int32)
    y_prompt, a_prompt, b_prompt = _trunk(x_prompt, pos_prompt, None, wts, w16)
    y_sample, a_sample, b_sample = _trunk(x_sample, pos_sample, (cache_a_k, cache_a_v, cache_b_k, cache_b_v), wts, w16)

    def heads(s, kvh):
        return s.reshape(*s.shape[:-1], kvh, HEAD_DIM)

    return (y_prompt, y_sample,
            jnp.stack([heads(s[0], kv_heads_a) for s in a_prompt], axis=0),
            jnp.stack([heads(s[1], kv_heads_a) for s in a_prompt], axis=0),
            jnp.stack([heads(s[0], kv_heads_a) for s in a_sample], axis=0),
            jnp.stack([heads(s[1], kv_heads_a) for s in a_sample], axis=0),
            heads(b_prompt[0], kv_heads_b), heads(b_prompt[1], kv_heads_b),
            heads(b_sample[0], kv_heads_b), heads(b_sample[1], kv_heads_b))
```
